```python
import math
import jax, jax.numpy as jnp
from jax import lax
import numpy as np

D_MODEL = 4096
BATCH = 2
SEQ = 8192
DEPTH = 2

CHUNK = 64
M_HEADS = 8
M_V_DIM = D_MODEL // (2 * M_HEADS)
M_QK_DIM = M_V_DIM // 2
M_WIDTH = M_HEADS * M_V_DIM
QK_WIDTH = M_HEADS * M_QK_DIM
CONV_WIDTH = 4
S5_WIDTH = D_MODEL // 4
S5_GROUP = 16
S5_GROUPS = S5_WIDTH // S5_GROUP
S5_STATE = 64
D_FF = 4 * D_MODEL
EPS = 1e-6
PROJ_WIDTH = 2 * QK_WIDTH + 2 * M_WIDTH + 2 * M_HEADS + S5_WIDTH + 2 * D_MODEL

kernel_name = 'hybrid_mlstm_s5_gated_block'


def rms_norm(x, gain):
    xf = x.astype(jnp.float32)
    y = xf * lax.rsqrt(jnp.mean(xf * xf, axis=-1, keepdims=True) + EPS)
    return (y * gain.astype(jnp.float32)).astype(x.dtype)


def causal_depthwise_conv(x, w):
    width = w.shape[0]
    ch = x.shape[-1]
    return lax.conv_general_dilated(
        x, w[:, None, :].astype(x.dtype), window_strides=(1,),
        padding=[(width - 1, 0)], dimension_numbers=('NWC', 'WIO', 'NWC'),
        feature_group_count=ch)


def mlstm_chunkwise(q, k, v, log_i, f_pre):
    bsz, nh, seq, dk = q.shape
    dv = v.shape[-1]
    nc = seq // CHUNK
    f32 = jnp.float32
    q = q.astype(f32).reshape(bsz, nh, nc, CHUNK, dk) * (dk ** -0.5)
    k = k.astype(f32).reshape(bsz, nh, nc, CHUNK, dk)
    v = v.astype(f32).reshape(bsz, nh, nc, CHUNK, dv)
    log_i = log_i.astype(f32).reshape(bsz, nh, nc, CHUNK)
    log_f = jax.nn.log_sigmoid(f_pre.astype(f32)).reshape(bsz, nh, nc, CHUNK)
    b = jnp.cumsum(log_f, axis=-1)
    g = b[..., -1]
    a = g[..., None] - b + log_i
    a_max = jnp.max(a, axis=-1)
    w_kv = jnp.exp(a - a_max[..., None])
    c_hat = jnp.einsum('bhcsd,bhcse->bhcde', w_kv[..., None] * k, v)
    n_hat = jnp.einsum('bhcs,bhcsd->bhcd', w_kv, k)

    def step(carry, xs):
        c_st, n_st, m_st = carry
        g_c, am_c, ch_c, nh_c = xs
        m_new = jnp.maximum(g_c + m_st, am_c)
        s_old = jnp.exp(g_c + m_st - m_new)
        s_new = jnp.exp(am_c - m_new)
        c_new = s_old[..., None, None] * c_st + s_new[..., None, None] * ch_c
        n_new = s_old[..., None] * n_st + s_new[..., None] * nh_c
        return (c_new, n_new, m_new), (c_st, n_st, m_st)

    init = (jnp.zeros((bsz, nh, dk, dv), f32), jnp.zeros((bsz, nh, dk), f32),
            jnp.zeros((bsz, nh), f32))
    xs = (jnp.moveaxis(g, -1, 0), jnp.moveaxis(a_max, -1, 0),
          jnp.moveaxis(c_hat, 2, 0), jnp.moveaxis(n_hat, 2, 0))
    _, (c_prev, n_prev, m_prev) = lax.scan(step, init, xs)
    c_prev = jnp.moveaxis(c_prev, 0, 2)
    n_prev = jnp.moveaxis(n_prev, 0, 2)
    m_prev = jnp.moveaxis(m_prev, 0, 2)

    causal = jnp.tril(jnp.ones((CHUNK, CHUNK), dtype=bool))
    d_mat = jnp.where(causal, b[..., :, None] - b[..., None, :] + log_i[..., None, :], -jnp.inf)
    m_inter = b + m_prev[..., None]
    m_t = jnp.maximum(m_inter, jnp.max(d_mat, axis=-1))
    p = jnp.exp(d_mat - m_t[..., None])
    s = jnp.einsum('bhctd,bhcsd->bhcts', q, k) * p
    scale_inter = jnp.exp(m_inter - m_t)
    num = (jnp.einsum('bhcts,bhcse->bhcte', s, v)
           + scale_inter[..., None] * jnp.einsum('bhctd,bhcde->bhcte', q, c_prev))
    den = jnp.sum(s, axis=-1) + scale_inter * jnp.einsum('bhctd,bhcd->bhct', q, n_prev)
    h = num / jnp.maximum(jnp.abs(den), jnp.exp(-m_t))[..., None]
    return h.reshape(bsz, nh, seq, dv)


def _complex_affine_combine(e1, e2):
    ar1, ai1, br1, bi1 = e1
    ar2, ai2, br2, bi2 = e2
    ar = ar2 * ar1 - ai2 * ai1
    ai = ar2 * ai1 + ai2 * ar1
    br = ar2 * br1 - ai2 * bi1 + br2
    bi = ar2 * bi1 + ai2 * br1 + bi2
    return (ar, ai, br, bi)


def s5_ssm(u, a_re, a_im, log_step, b_re, b_im, c_re, c_im, d_skip):
    bsz, seq, _ = u.shape
    f32 = jnp.float32
    u = u.astype(f32)
    lam_re = jnp.minimum(a_re.astype(f32), -1e-4)
    lam_im = a_im.astype(f32)
    dt = jnp.exp(log_step.astype(f32))[:, None]
    mag = jnp.exp(lam_re * dt)
    bar_re = mag * jnp.cos(lam_im * dt)
    bar_im = mag * jnp.sin(lam_im * dt)
    den = lam_re * lam_re + lam_im * lam_im
    nr = bar_re - 1.0
    r_re = (nr * lam_re + bar_im * lam_im) / den
    r_im = (bar_im * lam_re - nr * lam_im) / den
    b_re = b_re.astype(f32)
    b_im = b_im.astype(f32)
    bb_re = r_re[..., None] * b_re - r_im[..., None] * b_im
    bb_im = r_re[..., None] * b_im + r_im[..., None] * b_re
    ug = u.reshape(bsz, seq, S5_GROUPS, S5_GROUP)
    bu_re = jnp.einsum('bsgn,gpn->bsgp', ug, bb_re)
    bu_im = jnp.einsum('bsgn,gpn->bsgp', ug, bb_im)
    a_re_t = jnp.broadcast_to(bar_re, (1, seq, S5_GROUPS, S5_STATE))
    a_im_t = jnp.broadcast_to(bar_im, (1, seq, S5_GROUPS, S5_STATE))
    _, _, x_re, x_im = lax.associative_scan(
        _complex_affine_combine, (a_re_t, a_im_t, bu_re, bu_im), axis=1)
    y = (jnp.einsum('bsgp,gnp->bsgn', x_re, c_re.astype(f32))
         - jnp.einsum('bsgp,gnp->bsgn', x_im, c_im.astype(f32)))
    return y.reshape(bsz, seq, S5_WIDTH) + d_skip.astype(f32) * u


def hybrid_mixer(xn, w_in, conv_qk, b_igate, b_fgate, mh_norm_gain, p_a,
                 s5_a_re, s5_a_im, s5_log_step, s5_b_re, s5_b_im, s5_c_re, s5_c_im,
                 s5_d, w_glu, b_glu, p_b, w_out):
    bsz, seq, _ = xn.shape
    proj = xn @ w_in
    o1 = 2 * QK_WIDTH
    o2 = o1 + M_WIDTH
    o3 = o2 + M_WIDTH
    o4 = o3 + M_HEADS
    o5 = o4 + M_HEADS
    o6 = o5 + S5_WIDTH
    o7 = o6 + D_MODEL
    qk, v, o_pre, i_pre, f_pre, u, g_a, g_b = jnp.split(proj, [o1, o2, o3, o4, o5, o6, o7], axis=-1)

    qk = jax.nn.silu(causal_depthwise_conv(qk, conv_qk))
    q, k = jnp.split(qk, 2, axis=-1)

    def heads(t, d):
        return t.reshape(bsz, seq, M_HEADS, d).transpose(0, 2, 1, 3)

    h = mlstm_chunkwise(heads(q, M_QK_DIM), heads(k, M_QK_DIM), heads(v, M_V_DIM),
                        (i_pre + b_igate).transpose(0, 2, 1),
                        (f_pre + b_fgate).transpose(0, 2, 1))
    h = h.transpose(0, 2, 1, 3)
    mu = jnp.mean(h, axis=-1, keepdims=True)
    hc = h - mu
    var = jnp.mean(hc * hc, axis=-1, keepdims=True)
    h = hc * lax.rsqrt(var + EPS) * mh_norm_gain.astype(jnp.float32).reshape(M_HEADS, M_V_DIM)
    h = h.reshape(bsz, seq, M_WIDTH) * jax.nn.sigmoid(o_pre.astype(jnp.float32))
    y_a = h.astype(xn.dtype) @ p_a

    y_s = s5_ssm(u, s5_a_re, s5_a_im, s5_log_step, s5_b_re, s5_b_im, s5_c_re, s5_c_im, s5_d)
    z = jax.nn.gelu(y_s).astype(xn.dtype)
    z = z * jax.nn.sigmoid(z @ w_glu + b_glu)
    y_b = z @ p_b

    merged = jax.nn.sigmoid(g_a) * y_a + jax.nn.sigmoid(g_b) * y_b
    return merged @ w_out


def setup_inputs(seed: int = 0) -> dict:
    key = jax.random.key(seed)
    ks = jax.random.split(key, 26)
    f32 = jnp.float32

    def nrm(k, shape, scale):
        return jax.random.normal(k, shape, f32) * scale

    x = nrm(ks[0], (BATCH, SEQ, D_MODEL), 1.0)
    pre_mix_gain = 1.0 + nrm(ks[1], (DEPTH, D_MODEL), 0.05)
    post_mix_gain = 1.0 + nrm(ks[2], (DEPTH, D_MODEL), 0.05)
    pre_mlp_gain = 1.0 + nrm(ks[3], (DEPTH, D_MODEL), 0.05)
    post_mlp_gain = 1.0 + nrm(ks[4], (DEPTH, D_MODEL), 0.05)
    w_in = nrm(ks[5], (DEPTH, D_MODEL, PROJ_WIDTH), D_MODEL ** -0.5)
    conv_qk = nrm(ks[6], (DEPTH, CONV_WIDTH, 2 * QK_WIDTH), CONV_WIDTH ** -0.5)
    b_igate = nrm(ks[7], (DEPTH, M_HEADS), 0.1)
    b_fgate = jnp.linspace(3.0, 6.0, M_HEADS, dtype=f32)[None, :] + nrm(ks[8], (DEPTH, M_HEADS), 0.1)
    mh_norm_gain = 1.0 + nrm(ks[9], (DEPTH, M_WIDTH), 0.05)
    p_a = nrm(ks[10], (DEPTH, M_WIDTH, D_MODEL), M_WIDTH ** -0.5)
    s5_a_re = -0.5 + nrm(ks[11], (DEPTH, S5_GROUPS, S5_STATE), 0.01)
    s5_a_im = (math.pi * jnp.arange(S5_STATE, dtype=f32))[None, None, :] + nrm(ks[12], (DEPTH, S5_GROUPS, S5_STATE), 0.01)
    s5_log_step = jax.random.uniform(ks[13], (DEPTH, S5_GROUPS), f32, math.log(1e-3), math.log(1e-1))
    s5_b_re = nrm(ks[14], (DEPTH, S5_GROUPS, S5_STATE, S5_GROUP), (2 * S5_GROUP) ** -0.5)
    s5_b_im = nrm(ks[15], (DEPTH, S5_GROUPS, S5_STATE, S5_GROUP), (2 * S5_GROUP) ** -0.5)
    s5_c_re = nrm(ks[16], (DEPTH, S5_GROUPS, S5_GROUP, S5_STATE), S5_STATE ** -0.5)
    s5_c_im = nrm(ks[17], (DEPTH, S5_GROUPS, S5_GROUP, S5_STATE), S5_STATE ** -0.5)
    s5_d = nrm(ks[18], (DEPTH, S5_WIDTH), 1.0)
    w_glu = nrm(ks[19], (DEPTH, S5_WIDTH, S5_WIDTH), S5_WIDTH ** -0.5)
    b_glu = nrm(ks[20], (DEPTH, S5_WIDTH), 0.01)
    p_b = nrm(ks[21], (DEPTH, S5_WIDTH, D_MODEL), S5_WIDTH ** -0.5)
    w_out = nrm(ks[22], (DEPTH, D_MODEL, D_MODEL), D_MODEL ** -0.5)
    w_ff1 = nrm(ks[23], (DEPTH, D_MODEL, D_FF), D_MODEL ** -0.5)
    w_ff2 = nrm(ks[24], (DEPTH, D_FF, D_MODEL), D_FF ** -0.5)
    return {'x': x, 'pre_mix_gain': pre_mix_gain, 'post_mix_gain': post_mix_gain,
            'pre_mlp_gain': pre_mlp_gain, 'post_mlp_gain': post_mlp_gain,
            'w_in': w_in, 'conv_qk': conv_qk, 'b_igate': b_igate, 'b_fgate': b_fgate,
            'mh_norm_gain': mh_norm_gain, 'p_a': p_a,
            's5_a_re': s5_a_re, 's5_a_im': s5_a_im, 's5_log_step': s5_log_step,
            's5_b_re': s5_b_re, 's5_b_im': s5_b_im, 's5_c_re': s5_c_re, 's5_c_im': s5_c_im,
            's5_d': s5_d, 'w_glu': w_glu, 'b_glu': b_glu, 'p_b': p_b, 'w_out': w_out,
            'w_ff1': w_ff1, 'w_ff2': w_ff2}


def reference(x, pre_mix_gain, post_mix_gain, pre_mlp_gain, post_mlp_gain, w_in, conv_qk,
              b_igate, b_fgate, mh_norm_gain, p_a, s5_a_re, s5_a_im, s5_log_step,
              s5_b_re, s5_b_im, s5_c_re, s5_c_im, s5_d, w_glu, b_glu, p_b, w_out,
              w_ff1, w_ff2):
    for l in range(DEPTH):
        xn = rms_norm(x, pre_mix_gain[l])
        mix = hybrid_mixer(xn, w_in[l], conv_qk[l], b_igate[l], b_fgate[l], mh_norm_gain[l], p_a[l],
                           s5_a_re[l], s5_a_im[l], s5_log_step[l], s5_b_re[l], s5_b_im[l],
                           s5_c_re[l], s5_c_im[l], s5_d[l], w_glu[l], b_glu[l], p_b[l], w_out[l])
        x = x + rms_norm(mix, post_mix_gain[l])
        hn = rms_norm(x, pre_mlp_gain[l])
        ff = jnp.square(jax.nn.relu(hn @ w_ff1[l])) @ w_ff2[l]
        x = x + rms_norm(ff, post_mlp_gain[l])
    return x
```

```python
import functools
import math

import jax
import jax.numpy as jnp
from jax import lax
from jax.experimental import pallas as pl
from jax.experimental.pallas import tpu as pltpu

F32 = jnp.float32
BF16 = jnp.bfloat16

D_MODEL = 4096
M_HEADS = 8
M_V_DIM = D_MODEL // (2 * M_HEADS)
M_QK_DIM = M_V_DIM // 2
M_WIDTH = M_HEADS * M_V_DIM
QK_WIDTH = M_HEADS * M_QK_DIM
CONV_WIDTH = 4
S5_WIDTH = D_MODEL // 4
S5_GROUP = 16
S5_GROUPS = S5_WIDTH // S5_GROUP
S5_STATE = 64
S5_STATES = S5_GROUPS * S5_STATE
D_FF = 4 * D_MODEL
EPS = 1e-6

LANES = 128
SUBLANES = 8
VMEM_LIMIT = 56 * 1024 * 1024

PROJ_MAIN = 2 * QK_WIDTH + 2 * M_WIDTH + S5_WIDTH + 2 * D_MODEL
COL_V = 2 * QK_WIDTH
COL_O = COL_V + M_WIDTH
COL_U = COL_O + M_WIDTH
COL_GA = COL_U + S5_WIDTH
COL_GB = COL_GA + D_MODEL

MLSTM_CHUNK = 256
S5_BLOCK = 512
S5_SUB = 32
S5_NSUB = S5_BLOCK // S5_SUB
S5_TILE_STATES = 512
S5_TILES = S5_WIDTH // LANES


def _params(*sem):
    return pltpu.CompilerParams(dimension_semantics=sem, vmem_limit_bytes=VMEM_LIMIT)


def _rmsnorm_kernel(x_ref, g_ref, o_ref):
    x = x_ref[...]
    ms = jnp.mean(x * x, axis=-1, keepdims=True)
    o_ref[...] = (x * lax.rsqrt(ms + EPS) * g_ref[...]).astype(o_ref.dtype)


def _rmsnorm(x, gain, tm=256):
    t, d = x.shape
    return pl.pallas_call(
        _rmsnorm_kernel,
        grid=(t // tm,),
        in_specs=[pl.BlockSpec((tm, d), lambda i: (i, 0)),
                  pl.BlockSpec((1, d), lambda i: (0, 0))],
        out_specs=pl.BlockSpec((tm, d), lambda i: (i, 0)),
        out_shape=jax.ShapeDtypeStruct((t, d), BF16),
        compiler_params=_params("parallel"),
    )(x, gain.reshape(1, d))


def _norm_residual_kernel(y_ref, x_ref, gpost_ref, gnext_ref, xo_ref, hn_ref):
    y = y_ref[...].astype(F32)
    ms = jnp.mean(y * y, axis=-1, keepdims=True)
    xn = x_ref[...] + y * lax.rsqrt(ms + EPS) * gpost_ref[...]
    xo_ref[...] = xn
    if hn_ref is not None:
        ms2 = jnp.mean(xn * xn, axis=-1, keepdims=True)
        hn_ref[...] = (xn * lax.rsqrt(ms2 + EPS) * gnext_ref[...]).astype(hn_ref.dtype)


def _norm_residual_last_kernel(y_ref, x_ref, gpost_ref, xo_ref):
    _norm_residual_kernel(y_ref, x_ref, gpost_ref, None, xo_ref, None)


def _norm_residual(y, x, g_post, g_next, tm=256):
    t, d = x.shape
    row = pl.BlockSpec((tm, d), lambda i: (i, 0))
    vec = pl.BlockSpec((1, d), lambda i: (0, 0))
    if g_next is None:
        xo = pl.pallas_call(
            _norm_residual_last_kernel, grid=(t // tm,),
            in_specs=[row, row, vec], out_specs=row,
            out_shape=jax.ShapeDtypeStruct((t, d), F32),
            compiler_params=_params("parallel"),
        )(y, x, g_post.reshape(1, d))
        return xo, None
    return pl.pallas_call(
        _norm_residual_kernel, grid=(t // tm,),
        in_specs=[row, row, vec, vec], out_specs=[row, row],
        out_shape=[jax.ShapeDtypeStruct((t, d), F32), jax.ShapeDtypeStruct((t, d), BF16)],
        compiler_params=_params("parallel"),
    )(y, x, g_post.reshape(1, d), g_next.reshape(1, d))


def _matmul_kernel(a_ref, w_ref, o_ref, *, relu2):
    acc = jnp.dot(a_ref[...], w_ref[...], preferred_element_type=F32)
    if relu2:
        acc = jnp.square(jnp.maximum(acc, 0.0))
    o_ref[...] = acc.astype(o_ref.dtype)


def _matmul(a, w, out_dtype, relu2=False, tm=1024, tn=1024):
    m, k = a.shape
    n = w.shape[1]
    return pl.pallas_call(
        functools.partial(_matmul_kernel, relu2=relu2),
        grid=(m // tm, n // tn),
        in_specs=[pl.BlockSpec((tm, k), lambda i, j: (i, 0)),
                  pl.BlockSpec((k, tn), lambda i, j: (0, j))],
        out_specs=pl.BlockSpec((tm, tn), lambda i, j: (i, j)),
        out_shape=jax.ShapeDtypeStruct((m, n), out_dtype),
        compiler_params=_params("parallel", "parallel"),
    )(a, w)


def _matmul_acc_kernel(a_ref, w_ref, o_ref):
    @pl.when(pl.program_id(1) == 0)
    def _():
        o_ref[...] = jnp.zeros_like(o_ref)

    o_ref[...] += jnp.dot(a_ref[...], w_ref[...], preferred_element_type=F32)


def _matmul_acc(a, w, tm=512, tk=1024):
    m, k = a.shape
    n = w.shape[1]
    return pl.pallas_call(
        _matmul_acc_kernel,
        grid=(m // tm, k // tk),
        in_specs=[pl.BlockSpec((tm, tk), lambda i, kk: (i, kk)),
                  pl.BlockSpec((tk, n), lambda i, kk: (kk, 0))],
        out_specs=pl.BlockSpec((tm, n), lambda i, kk: (i, 0)),
        out_shape=jax.ShapeDtypeStruct((m, n), F32),
        compiler_params=_params("parallel", "arbitrary"),
    )(a, w)


def _gates_kernel(xn_ref, wg_ref, wgt_ref, bcol_ref, brow_ref, col_ref, row_ref):
    xn = xn_ref[...]

    def finish(z, is_f):
        ls = jnp.minimum(z, 0.0) - jnp.log1p(jnp.exp(-jnp.abs(z)))
        return jnp.where(is_f, ls, z)

    zc = jnp.dot(xn, wg_ref[...], preferred_element_type=F32) + bcol_ref[...]
    lane = lax.broadcasted_iota(jnp.int32, zc.shape, 1)
    col_ref[...] = finish(zc, lane >= M_HEADS)
    zr = lax.dot_general(wgt_ref[...], xn, (((1,), (1,)), ((), ())),
                         preferred_element_type=F32) + brow_ref[...]
    sub = lax.broadcasted_iota(jnp.int32, zr.shape, 0)
    row_ref[...] = finish(zr, sub >= M_HEADS)


def _gates(xn, wg, wgt, bcol, brow, tm=512):
    t, d = xn.shape
    return pl.pallas_call(
        _gates_kernel, grid=(t // tm,),
        in_specs=[pl.BlockSpec((tm, d), lambda i: (i, 0)),
                  pl.BlockSpec((d, LANES), lambda i: (0, 0)),
                  pl.BlockSpec((2 * M_HEADS, d), lambda i: (0, 0)),
                  pl.BlockSpec((1, LANES), lambda i: (0, 0)),
                  pl.BlockSpec((2 * M_HEADS, 1), lambda i: (0, 0))],
        out_specs=[pl.BlockSpec((tm, LANES), lambda i: (i, 0)),
                   pl.BlockSpec((2 * M_HEADS, tm), lambda i: (0, i))],
        out_shape=[jax.ShapeDtypeStruct((t, LANES), F32),
                   jax.ShapeDtypeStruct((2 * M_HEADS, t), F32)],
        compiler_params=_params("parallel"),
    )(xn, wg, wgt, bcol, brow)


def _cumsum_rows(x):
    n = x.shape[0]
    row = lax.broadcasted_iota(jnp.int32, x.shape, 0)
    k = 1
    while k < n:
        x = x + jnp.where(row >= k, pltpu.roll(x, k, 0), 0.0)
        k *= 2
    return x


def _cumsum_lanes(x):
    n = x.shape[1]
    col = lax.broadcasted_iota(jnp.int32, x.shape, 1)
    k = 1
    while k < n:
        x = x + jnp.where(col >= k, pltpu.roll(x, k, 1), 0.0)
        k *= 2
    return x


def _mlstm_kernel(qk_ref, v_ref, o_ref, gc_ref, gr_ref, cw_ref, gain_ref, out_ref,
                  c_scr, n_scr, m_scr, tail_scr):
    L = qk_ref.shape[0]

    @pl.when(pl.program_id(1) == 0)
    def _():
        c_scr[...] = jnp.zeros_like(c_scr)
        n_scr[...] = jnp.zeros_like(n_scr)
        m_scr[...] = jnp.zeros_like(m_scr)
        tail_scr[...] = jnp.zeros_like(tail_scr)

    x = qk_ref[...].astype(F32)
    tail = tail_scr[...]
    row8 = lax.broadcasted_iota(jnp.int32, tail.shape, 0)
    acc = x * cw_ref[CONV_WIDTH - 1:CONV_WIDTH, :]
    for sh in range(1, CONV_WIDTH):
        r = pltpu.roll(x, sh, 0)
        head = jnp.where(row8 < sh, pltpu.roll(tail, sh, 0), r[0:SUBLANES])
        r = jnp.concatenate([head, r[SUBLANES:]], axis=0)
        acc = acc + r * cw_ref[CONV_WIDTH - 1 - sh:CONV_WIDTH - sh, :]
    tail_scr[...] = x[L - SUBLANES:L]
    qk = acc * (1.0 / (1.0 + jnp.exp(-acc)))
    q_all = (qk[:, :QK_WIDTH] * (M_QK_DIM ** -0.5)).astype(BF16)
    k_all = qk[:, QK_WIDTH:]

    gc = gc_ref[...]
    gr = gr_ref[...]
    bc_all = _cumsum_rows(gc)
    br_all = _cumsum_lanes(gr)
    ti = lax.broadcasted_iota(jnp.int32, (L, L), 0)
    si = lax.broadcasted_iota(jnp.int32, (L, L), 1)
    causal = ti >= si

    for h in range(M_HEADS):
        li_col = gc[:, h:h + 1]
        b_col = bc_all[:, M_HEADS + h:M_HEADS + h + 1]
        li_row = gr[h:h + 1, :]
        b_row = br_all[M_HEADS + h:M_HEADS + h + 1, :]
        g_tot = b_row[:, L - 1:L]
        m_prev = m_scr[h:h + 1, 0:1]
        qh = q_all[:, h * M_QK_DIM:(h + 1) * M_QK_DIM]
        kh = k_all[:, h * M_QK_DIM:(h + 1) * M_QK_DIM]
        vh = v_ref[:, h * M_V_DIM:(h + 1) * M_V_DIM]
        c_prev = c_scr[h]
        n_prev = n_scr[h:h + 1, :]

        dmat = jnp.where(causal, b_col - b_row + li_row, -jnp.inf)
        m_inter = b_col + m_prev
        m_t = jnp.maximum(m_inter, jnp.max(dmat, axis=1, keepdims=True))
        p = jnp.exp(dmat - m_t)
        s = lax.dot_general(qh, kh.astype(BF16), (((1,), (1,)), ((), ())),
                            preferred_element_type=F32) * p
        sc = jnp.exp(m_inter - m_t)
        num = (jnp.dot(s.astype(BF16), vh, preferred_element_type=F32)
               + sc * jnp.dot(qh, c_prev.astype(BF16), preferred_element_type=F32))
        den = (jnp.sum(s, axis=1, keepdims=True)
               + sc * jnp.sum(qh.astype(F32) * n_prev, axis=1, keepdims=True))
        hh = num / jnp.maximum(jnp.abs(den), jnp.exp(-m_t))

        a_col = g_tot - b_col + li_col
        m_new = jnp.maximum(g_tot + m_prev, jnp.max(a_col, axis=0, keepdims=True))
        kw = kh * jnp.exp(a_col - m_new)
        decay = jnp.exp(g_tot + m_prev - m_new)
        c_scr[h] = decay * c_prev + lax.dot_general(
            kw.astype(BF16), vh, (((0,), (0,)), ((), ())), preferred_element_type=F32)
        n_scr[h:h + 1, :] = decay * n_prev + jnp.sum(kw, axis=0, keepdims=True)
        m_scr[h:h + 1, :] = jnp.broadcast_to(m_new, (1, LANES))

        mu = jnp.mean(hh, axis=1, keepdims=True)
        hc = hh - mu
        var = jnp.mean(hc * hc, axis=1, keepdims=True)
        cols = slice(h * M_V_DIM, (h + 1) * M_V_DIM)
        og = o_ref[:, cols].astype(F32)
        hn = hc * lax.rsqrt(var + EPS) * gain_ref[:, cols] * (1.0 / (1.0 + jnp.exp(-og)))
        out_ref[:, cols] = hn.astype(out_ref.dtype)


def _mlstm(proj, gates_col, gates_row, conv_w, gain, batch, seq):
    L = MLSTM_CHUNK
    nc = seq // L
    row = lambda b, c: b * nc + c
    return pl.pallas_call(
        _mlstm_kernel,
        grid=(batch, nc),
        in_specs=[
            pl.BlockSpec((L, 2 * QK_WIDTH), lambda b, c: (row(b, c), 0)),
            pl.BlockSpec((L, M_WIDTH), lambda b, c: (row(b, c), COL_V // M_WIDTH)),
            pl.BlockSpec((L, M_WIDTH), lambda b, c: (row(b, c), COL_O // M_WIDTH)),
            pl.BlockSpec((L, LANES), lambda b, c: (row(b, c), 0)),
            pl.BlockSpec((2 * M_HEADS, L), lambda b, c: (0, row(b, c))),
            pl.BlockSpec((CONV_WIDTH, 2 * QK_WIDTH), lambda b, c: (0, 0)),
            pl.BlockSpec((1, M_WIDTH), lambda b, c: (0, 0)),
        ],
        out_specs=pl.BlockSpec((L, M_WIDTH), lambda b, c: (row(b, c), 0)),
        out_shape=jax.ShapeDtypeStruct((batch * seq, M_WIDTH), BF16),
        scratch_shapes=[
            pltpu.VMEM((M_HEADS, M_QK_DIM, M_V_DIM), F32),
            pltpu.VMEM((M_HEADS, M_QK_DIM), F32),
            pltpu.VMEM((M_HEADS, LANES), F32),
            pltpu.VMEM((SUBLANES, 2 * QK_WIDTH), F32),
        ],
        compiler_params=_params("parallel", "arbitrary"),
    )(proj, proj, proj, gates_col, gates_row, conv_w, gain.reshape(1, M_WIDTH))


def _s5_kernel(u_ref, perm_ref, permt_ref, bbar_ref, cmat_ref, lam_ref, lamsub_ref, pow_ref,
               d_ref, wglu_ref, bglu_ref, out_ref, st_scr, x0_scr, bu_scr, xs_scr, y_scr):
    ns, nsub, w = S5_SUB, S5_NSUB, S5_TILE_STATES

    @pl.when(pl.program_id(1) == 0)
    def _():
        st_scr[...] = jnp.zeros_like(st_scr)

    up = jnp.dot(perm_ref[...], u_ref[...], preferred_element_type=F32).astype(BF16)

    for j in range(S5_TILES):
        ut = up[:, j * LANES:(j + 1) * LANES]
        bu_scr[...] = jnp.dot(ut, bbar_ref[j], preferred_element_type=F32)
        lr = lam_ref[0:1, j * w:(j + 1) * w]
        li = lam_ref[1:2, j * w:(j + 1) * w]

        def local_step(s, carry):
            xr, xi = carry
            rows = pl.ds(pl.multiple_of(s * nsub, nsub), nsub)
            nr = lr * xr - li * xi + bu_scr[rows, 0:w]
            ni = lr * xi + li * xr + bu_scr[rows, w:2 * w]
            bu_scr[rows, 0:w] = nr
            bu_scr[rows, w:2 * w] = ni
            return nr, ni

        zero = jnp.zeros((nsub, w), F32)
        er, ei = lax.fori_loop(0, ns, local_step, (zero, zero))

        x0_scr[0:nsub, :] = er
        x0_scr[nsub:2 * nsub, :] = ei
        lsr = lamsub_ref[0:1, j * w:(j + 1) * w]
        lsi = lamsub_ref[1:2, j * w:(j + 1) * w]

        def chain_step(c, carry):
            xr, xi = carry
            e_r = x0_scr[pl.ds(c, 1), :]
            e_i = x0_scr[pl.ds(nsub + c, 1), :]
            x0_scr[pl.ds(c, 1), :] = xr
            x0_scr[pl.ds(nsub + c, 1), :] = xi
            return lsr * xr - lsi * xi + e_r, lsr * xi + lsi * xr + e_i

        sr, si = lax.fori_loop(0, nsub, chain_step,
                               (st_scr[0:1, j * w:(j + 1) * w], st_scr[1:2, j * w:(j + 1) * w]))
        st_scr[0:1, j * w:(j + 1) * w] = sr
        st_scr[1:2, j * w:(j + 1) * w] = si
        x0r = x0_scr[0:nsub, :]
        x0i = x0_scr[nsub:2 * nsub, :]

        def fix_step(s, carry):
            rows = pl.ds(pl.multiple_of(s * nsub, nsub), nsub)
            pr = pow_ref[pl.ds(s, 1), j * w:(j + 1) * w]
            pi = pow_ref[pl.ds(ns + s, 1), j * w:(j + 1) * w]
            xs_scr[rows, 0:w] = (bu_scr[rows, 0:w] + pr * x0r - pi * x0i).astype(BF16)
            xs_scr[rows, w:2 * w] = (bu_scr[rows, w:2 * w] + pr * x0i + pi * x0r).astype(BF16)
            return carry

        lax.fori_loop(0, ns, fix_step, 0)
        y_scr[:, j * LANES:(j + 1) * LANES] = (
            jnp.dot(xs_scr[...], cmat_ref[j], preferred_element_type=F32)
            + d_ref[:, j * LANES:(j + 1) * LANES] * ut.astype(F32))

    y = y_scr[...]
    z = 0.5 * y * (1.0 + jnp.tanh(math.sqrt(2.0 / math.pi) * (y + 0.044715 * (y * y * y))))
    gate = jnp.dot(z.astype(BF16), wglu_ref[...], preferred_element_type=F32) + bglu_ref[...]
    zg = (z * (1.0 / (1.0 + jnp.exp(-gate)))).astype(BF16)
    out_ref[...] = jnp.dot(permt_ref[...], zg, preferred_element_type=F32).astype(out_ref.dtype)


def _s5(proj, perm, permt, bbar, cmat, lam, lamsub, pows, d_skip, w_glu, b_glu, batch, seq):
    tb = S5_BLOCK
    nb = seq // tb
    full = lambda a: pl.BlockSpec(a.shape, lambda b, c: (0,) * a.ndim)
    return pl.pallas_call(
        _s5_kernel,
        grid=(batch, nb),
        in_specs=[pl.BlockSpec((tb, S5_WIDTH), lambda b, c: (b * nb + c, COL_U // S5_WIDTH)),
                  full(perm), full(permt), full(bbar), full(cmat), full(lam), full(lamsub),
                  full(pows), full(d_skip), full(w_glu), full(b_glu)],
        out_specs=pl.BlockSpec((tb, S5_WIDTH), lambda b, c: (b * nb + c, 0)),
        out_shape=jax.ShapeDtypeStruct((batch * seq, S5_WIDTH), BF16),
        scratch_shapes=[
            pltpu.VMEM((2, S5_STATES), F32),
            pltpu.VMEM((2 * S5_NSUB, S5_TILE_STATES), F32),
            pltpu.VMEM((tb, 2 * S5_TILE_STATES), F32),
            pltpu.VMEM((tb, 2 * S5_TILE_STATES), BF16),
            pltpu.VMEM((tb, S5_WIDTH), F32),
        ],
        compiler_params=_params("parallel", "arbitrary"),
    )(proj, perm, permt, bbar, cmat, lam, lamsub, pows, d_skip, w_glu, b_glu)


def _s5_tables(a_re, a_im, log_step, b_re, b_im, c_re, c_im):
    lam_re = jnp.minimum(a_re, -1e-4)
    lam_im = a_im
    dt = jnp.exp(log_step)[:, None]
    mag = jnp.exp(lam_re * dt)
    bar_re = mag * jnp.cos(lam_im * dt)
    bar_im = mag * jnp.sin(lam_im * dt)
    den = lam_re * lam_re + lam_im * lam_im
    nr = bar_re - 1.0
    r_re = (nr * lam_re + bar_im * lam_im) / den
    r_im = (bar_im * lam_re - nr * lam_im) / den
    bb_re = r_re[..., None] * b_re - r_im[..., None] * b_im
    bb_im = r_re[..., None] * b_im + r_im[..., None] * b_re

    gpt = S5_GROUPS // S5_TILES
    eye = jnp.eye(gpt, dtype=F32)

    def in_tile(bb):
        t = bb.reshape(S5_TILES, gpt, S5_STATE, S5_GROUP)
        t = jnp.einsum('jgpn,gh->jgnhp', t, eye)
        return t.reshape(S5_TILES, LANES, S5_TILE_STATES)

    def out_tile(c):
        t = c.reshape(S5_TILES, gpt, S5_GROUP, S5_STATE)
        t = jnp.einsum('jgnp,gh->jgphn', t, eye)
        return t.reshape(S5_TILES, S5_TILE_STATES, LANES)

    bbar = jnp.concatenate([in_tile(bb_re), in_tile(bb_im)], axis=2).astype(BF16)
    cmat = jnp.concatenate([out_tile(c_re), -out_tile(c_im)], axis=1).astype(BF16)
    lam = jnp.stack([bar_re.reshape(-1), bar_im.reshape(-1)])

    def power(k):
        mk = jnp.exp(k * lam_re * dt)
        return (mk * jnp.cos(k * lam_im * dt)).reshape(-1), (mk * jnp.sin(k * lam_im * dt)).reshape(-1)

    lamsub = jnp.stack(power(float(S5_SUB)))
    steps = [power(float(s + 1)) for s in range(S5_SUB)]
    pows = jnp.stack([p[0] for p in steps] + [p[1] for p in steps])
    return bbar, cmat, lam, lamsub, pows


def _s5_perm():
    r = jnp.arange(S5_BLOCK)
    src = (r % S5_NSUB) * S5_SUB + r // S5_NSUB
    perm = (src[:, None] == jnp.arange(S5_BLOCK)[None, :]).astype(BF16)
    return perm, perm.T


def _merge_kernel(ha_ref, zb_ref, pa_ref, pb_ref, ga_ref, gb_ref, o_ref):
    ya = jnp.dot(ha_ref[...], pa_ref[...], preferred_element_type=F32)
    yb = jnp.dot(zb_ref[...], pb_ref[...], preferred_element_type=F32)
    sa = 1.0 / (1.0 + jnp.exp(-ga_ref[...].astype(F32)))
    sb = 1.0 / (1.0 + jnp.exp(-gb_ref[...].astype(F32)))
    o_ref[...] = (sa * ya + sb * yb).astype(o_ref.dtype)


def _merge(ha, zb, p_a, p_b, proj, tm=512, tn=1024):
    t = ha.shape[0]
    return pl.pallas_call(
        _merge_kernel,
        grid=(t // tm, D_MODEL // tn),
        in_specs=[pl.BlockSpec((tm, M_WIDTH), lambda i, j: (i, 0)),
                  pl.BlockSpec((tm, S5_WIDTH), lambda i, j: (i, 0)),
                  pl.BlockSpec((M_WIDTH, tn), lambda i, j: (0, j)),
                  pl.BlockSpec((S5_WIDTH, tn), lambda i, j: (0, j)),
                  pl.BlockSpec((tm, tn), lambda i, j: (i, COL_GA // tn + j)),
                  pl.BlockSpec((tm, tn), lambda i, j: (i, COL_GB // tn + j))],
        out_specs=pl.BlockSpec((tm, tn), lambda i, j: (i, j)),
        out_shape=jax.ShapeDtypeStruct((t, D_MODEL), BF16),
        compiler_params=_params("parallel", "parallel"),
    )(ha, zb, p_a, p_b, proj, proj)


def _split_w_in(w):
    o1 = 2 * QK_WIDTH
    o2 = o1 + M_WIDTH
    o3 = o2 + M_WIDTH
    o5 = o3 + 2 * M_HEADS
    main = jnp.concatenate([w[:, :o3], w[:, o5:]], axis=1).astype(BF16)
    wg = w[:, o3:o5]
    wg_col = jnp.pad(wg, ((0, 0), (0, LANES - 2 * M_HEADS))).astype(BF16)
    wg_row = wg.T.astype(BF16)
    return main, wg_col, wg_row


def kernel(x, pre_mix_gain, post_mix_gain, pre_mlp_gain, post_mlp_gain, w_in, conv_qk, b_igate, b_fgate, mh_norm_gain, p_a, s5_a_re, s5_a_im, s5_log_step, s5_b_re, s5_b_im, s5_c_re, s5_c_im, s5_d, w_glu, b_glu, p_b, w_out, w_ff1, w_ff2):
    batch, seq, d = x.shape
    depth = w_in.shape[0]
    t = batch * seq
    xf = x.reshape(t, d)
    perm, permt = _s5_perm()
    xn = _rmsnorm(xf, pre_mix_gain[0])
    for l in range(depth):
        w_main, wg_col, wg_row = _split_w_in(w_in[l])
        bias = jnp.concatenate([b_igate[l], b_fgate[l]])
        bcol = jnp.pad(bias, (0, LANES - 2 * M_HEADS)).reshape(1, LANES)
        brow = bias.reshape(2 * M_HEADS, 1)

        proj = _matmul(xn, w_main, BF16)
        gates_col, gates_row = _gates(xn, wg_col, wg_row, bcol, brow)
        ha = _mlstm(proj, gates_col, gates_row, conv_qk[l], mh_norm_gain[l], batch, seq)
        tables = _s5_tables(s5_a_re[l], s5_a_im[l], s5_log_step[l], s5_b_re[l], s5_b_im[l],
                            s5_c_re[l], s5_c_im[l])
        zb = _s5(proj, perm, permt, *tables, s5_d[l].reshape(1, S5_WIDTH), w_glu[l].astype(BF16),
                 b_glu[l].reshape(1, S5_WIDTH), batch, seq)
        merged = _merge(ha, zb, p_a[l].astype(BF16), p_b[l].astype(BF16), proj)
        mix = _matmul(merged, w_out[l].astype(BF16), F32)
        xf, hn = _norm_residual(mix, xf, post_mix_gain[l], pre_mlp_gain[l])
        hid = _matmul(hn, w_ff1[l].astype(BF16), BF16, relu2=True)
        ff = _matmul_acc(hid, w_ff2[l].astype(BF16))
        g_next = pre_mix_gain[l + 1] if l + 1 < depth else None
        xf, xn = _norm_residual(ff, xf, post_mlp_gain[l], g_next)
    return xf.reshape(batch, seq, d)
```

```python
import functools
import math

import jax
import jax.numpy as jnp
from jax import lax
from jax.experimental import pallas as pl
from jax.experimental.pallas import tpu as pltpu

F32 = jnp.float32
BF16 = jnp.bfloat16

D_MODEL = 4096
M_HEADS = 8
M_V_DIM = D_MODEL // (2 * M_HEADS)
M_QK_DIM = M_V_DIM // 2
M_WIDTH = M_HEADS * M_V_DIM
QK_WIDTH = M_HEADS * M_QK_DIM
CONV_WIDTH = 4
S5_WIDTH = D_MODEL // 4
S5_GROUP = 16
S5_GROUPS = S5_WIDTH // S5_GROUP
S5_STATE = 64
S5_STATES = S5_GROUPS * S5_STATE
D_FF = 4 * D_MODEL
EPS = 1e-6

LANES = 128
SUBLANES = 8
VMEM_LIMIT = 56 * 1024 * 1024

PROJ_MAIN = 2 * QK_WIDTH + 2 * M_WIDTH + S5_WIDTH + 2 * D_MODEL
COL_V = 2 * QK_WIDTH
COL_O = COL_V + M_WIDTH
COL_U = COL_O + M_WIDTH
COL_GA = COL_U + S5_WIDTH
COL_GB = COL_GA + D_MODEL

MLSTM_CHUNK = 256
S5_BLOCK = 512
S5_SUB = 32
S5_NSUB = S5_BLOCK // S5_SUB
S5_TILE_STATES = 512
S5_TILES = S5_WIDTH // LANES


def _params(*sem):
    return pltpu.CompilerParams(dimension_semantics=sem, vmem_limit_bytes=VMEM_LIMIT)


def _sigmoid(x):
    return 1.0 / (1.0 + jnp.exp(-x))


def _rmsnorm_kernel(x_ref, g_ref, o_ref):
    x = x_ref[...]
    ms = jnp.mean(x * x, axis=-1, keepdims=True)
    o_ref[...] = (x * lax.rsqrt(ms + EPS) * g_ref[...]).astype(o_ref.dtype)


def _rmsnorm(x, gain, tm=256):
    t, d = x.shape
    return pl.pallas_call(
        _rmsnorm_kernel,
        grid=(t // tm,),
        in_specs=[pl.BlockSpec((tm, d), lambda i: (i, 0)),
                  pl.BlockSpec((1, d), lambda i: (0, 0))],
        out_specs=pl.BlockSpec((tm, d), lambda i: (i, 0)),
        out_shape=jax.ShapeDtypeStruct((t, d), BF16),
        compiler_params=_params("parallel"),
        name="rmsnorm",
    )(x, gain.reshape(1, d))


def _norm_residual_kernel(y_ref, x_ref, gpost_ref, gnext_ref, xo_ref, hn_ref):
    y = y_ref[...].astype(F32)
    ms = jnp.mean(y * y, axis=-1, keepdims=True)
    xn = x_ref[...] + y * lax.rsqrt(ms + EPS) * gpost_ref[...]
    xo_ref[...] = xn
    if hn_ref is not None:
        ms2 = jnp.mean(xn * xn, axis=-1, keepdims=True)
        hn_ref[...] = (xn * lax.rsqrt(ms2 + EPS) * gnext_ref[...]).astype(hn_ref.dtype)


def _norm_residual_last_kernel(y_ref, x_ref, gpost_ref, xo_ref):
    _norm_residual_kernel(y_ref, x_ref, gpost_ref, None, xo_ref, None)


def _norm_residual(y, x, g_post, g_next, tm=256):
    t, d = x.shape
    row = pl.BlockSpec((tm, d), lambda i: (i, 0))
    vec = pl.BlockSpec((1, d), lambda i: (0, 0))
    if g_next is None:
        xo = pl.pallas_call(
            _norm_residual_last_kernel, grid=(t // tm,),
            in_specs=[row, row, vec], out_specs=row,
            out_shape=jax.ShapeDtypeStruct((t, d), F32),
            compiler_params=_params("parallel"),
            name="norm_residual_last",
        )(y, x, g_post.reshape(1, d))
        return xo, None
    return pl.pallas_call(
        _norm_residual_kernel, grid=(t // tm,),
        in_specs=[row, row, vec, vec], out_specs=[row, row],
        out_shape=[jax.ShapeDtypeStruct((t, d), F32), jax.ShapeDtypeStruct((t, d), BF16)],
        compiler_params=_params("parallel"),
        name="norm_residual",
    )(y, x, g_post.reshape(1, d), g_next.reshape(1, d))


def _log_sigmoid(z):
    return jnp.minimum(z, 0.0) - jnp.log1p(jnp.exp(-jnp.abs(z)))


def _proj_kernel(xn_ref, w_ref, wg_ref, wgt_ref, bcol_ref, brow_ref, o_ref, col_ref, row_ref):
    xn = xn_ref[...]
    o_ref[...] = jnp.dot(xn, w_ref[...], preferred_element_type=F32).astype(o_ref.dtype)

    @pl.when(pl.program_id(1) == 0)
    def _():
        zc = jnp.dot(xn, wg_ref[...], preferred_element_type=F32) + bcol_ref[...]
        lane = lax.broadcasted_iota(jnp.int32, zc.shape, 1)
        col_ref[...] = jnp.where(lane >= M_HEADS, _log_sigmoid(zc), zc)
        zr = lax.dot_general(wgt_ref[...], xn, (((1,), (1,)), ((), ())),
                             preferred_element_type=F32) + brow_ref[...]
        sub = lax.broadcasted_iota(jnp.int32, zr.shape, 0)
        row_ref[...] = jnp.where(sub >= M_HEADS, _log_sigmoid(zr), zr)


def _proj(xn, w, wg, wgt, bcol, brow, tm=1024, tn=1024):
    t, d = xn.shape
    n = w.shape[1]
    const = lambda shape: pl.BlockSpec(shape, lambda i, j: (0, 0))
    return pl.pallas_call(
        _proj_kernel,
        grid=(t // tm, n // tn),
        in_specs=[pl.BlockSpec((tm, d), lambda i, j: (i, 0)),
                  pl.BlockSpec((d, tn), lambda i, j: (0, j)),
                  const((d, LANES)), const((2 * M_HEADS, d)),
                  const((1, LANES)), const((2 * M_HEADS, 1))],
        out_specs=[pl.BlockSpec((tm, tn), lambda i, j: (i, j)),
                   pl.BlockSpec((tm, LANES), lambda i, j: (i, 0)),
                   pl.BlockSpec((2 * M_HEADS, tm), lambda i, j: (0, i))],
        out_shape=[jax.ShapeDtypeStruct((t, n), BF16),
                   jax.ShapeDtypeStruct((t, LANES), F32),
                   jax.ShapeDtypeStruct((2 * M_HEADS, t), F32)],
        compiler_params=_params("parallel", "arbitrary"),
        name="proj",
    )(xn, w, wg, wgt, bcol, brow)


def _matmul_kernel(a_ref, w_ref, o_ref):
    o_ref[...] = jnp.dot(a_ref[...], w_ref[...], preferred_element_type=F32).astype(o_ref.dtype)


def _matmul(a, w, out_dtype, tm, tn, name):
    m, k = a.shape
    n = w.shape[1]
    return pl.pallas_call(
        _matmul_kernel,
        grid=(m // tm, n // tn),
        in_specs=[pl.BlockSpec((tm, k), lambda i, j: (i, 0)),
                  pl.BlockSpec((k, tn), lambda i, j: (0, j))],
        out_specs=pl.BlockSpec((tm, tn), lambda i, j: (i, j)),
        out_shape=jax.ShapeDtypeStruct((m, n), out_dtype),
        compiler_params=_params("parallel", "parallel"),
        name=name,
    )(a, w)


def _matmul_wcast_kernel(a_ref, w_ref, *rest, relu2, ride):
    if ride:
        w2_ref, o_ref, w2b_ref, wb_scr = rest
        w2b_ref[...] = w2_ref[...].astype(BF16)
    else:
        o_ref, wb_scr = rest

    @pl.when(pl.program_id(1) == 0)
    def _():
        wb_scr[...] = w_ref[...].astype(BF16)

    acc = jnp.dot(a_ref[...], wb_scr[...], preferred_element_type=F32)
    if relu2:
        acc = jnp.square(jnp.maximum(acc, 0.0))
    o_ref[...] = acc.astype(o_ref.dtype)


def _matmul_wcast(a, w_stack, layer, name, relu2=False, ride_stack=None, tm=1024, tn=512):
    m, k = a.shape
    n = w_stack.shape[2]
    nj, ni = n // tn, m // tm
    in_specs = [pl.BlockSpec((tm, k), lambda j, i: (i, 0)),
                pl.BlockSpec((None, k, tn), lambda j, i: (layer, 0, j))]
    out_specs = [pl.BlockSpec((tm, tn), lambda j, i: (i, j))]
    out_shape = [jax.ShapeDtypeStruct((m, n), BF16)]
    args = [a, w_stack]
    if ride_stack is not None:
        r, c = ride_stack.shape[1:]
        rb = r // (nj * ni)
        assert rb * nj * ni == r and rb % (2 * SUBLANES) == 0
        in_specs.append(pl.BlockSpec((None, rb, c), lambda j, i: (layer, j * ni + i, 0)))
        out_specs.append(pl.BlockSpec((rb, c), lambda j, i: (j * ni + i, 0)))
        out_shape.append(jax.ShapeDtypeStruct((r, c), BF16))
        args.append(ride_stack)
    out = pl.pallas_call(
        functools.partial(_matmul_wcast_kernel, relu2=relu2, ride=ride_stack is not None),
        grid=(nj, ni),
        in_specs=in_specs, out_specs=out_specs, out_shape=out_shape,
        scratch_shapes=[pltpu.VMEM((k, tn), BF16)],
        compiler_params=_params("parallel", "arbitrary"),
        name=name,
    )(*args)
    return out if ride_stack is not None else out[0]


def _cumsum_rows(x):
    n = x.shape[0]
    row = lax.broadcasted_iota(jnp.int32, x.shape, 0)
    k = 1
    while k < n:
        x = x + jnp.where(row >= k, pltpu.roll(x, k, 0), 0.0)
        k *= 2
    return x


def _cumsum_lanes(x):
    n = x.shape[1]
    col = lax.broadcasted_iota(jnp.int32, x.shape, 1)
    k = 1
    while k < n:
        x = x + jnp.where(col >= k, pltpu.roll(x, k, 1), 0.0)
        k *= 2
    return x


def _mlstm_kernel(qk_ref, v_ref, o_ref, gc_ref, gr_ref, cw_ref, gain_ref, out_ref,
                  c_scr, n_scr, m_scr, tail_scr):
    L = qk_ref.shape[0]

    @pl.when(pl.program_id(1) == 0)
    def _():
        c_scr[...] = jnp.zeros_like(c_scr)
        n_scr[...] = jnp.zeros_like(n_scr)
        m_scr[...] = jnp.zeros_like(m_scr)
        tail_scr[...] = jnp.zeros_like(tail_scr)

    x = qk_ref[...].astype(F32)
    tail = tail_scr[...]
    row8 = lax.broadcasted_iota(jnp.int32, tail.shape, 0)
    acc = x * cw_ref[CONV_WIDTH - 1:CONV_WIDTH, :]
    for sh in range(1, CONV_WIDTH):
        r = pltpu.roll(x, sh, 0)
        head = jnp.where(row8 < sh, pltpu.roll(tail, sh, 0), r[0:SUBLANES])
        r = jnp.concatenate([head, r[SUBLANES:]], axis=0)
        acc = acc + r * cw_ref[CONV_WIDTH - 1 - sh:CONV_WIDTH - sh, :]
    tail_scr[...] = x[L - SUBLANES:L]
    qk = acc * _sigmoid(acc)
    q_all = (qk[:, :QK_WIDTH] * (M_QK_DIM ** -0.5)).astype(BF16)
    k_all = qk[:, QK_WIDTH:]

    gc = gc_ref[...]
    gr = gr_ref[...]
    bc_all = _cumsum_rows(gc)
    br_all = _cumsum_lanes(gr)
    ti = lax.broadcasted_iota(jnp.int32, (L, L), 0)
    si = lax.broadcasted_iota(jnp.int32, (L, L), 1)
    causal = ti >= si

    for h in range(M_HEADS):
        li_col = gc[:, h:h + 1]
        b_col = bc_all[:, M_HEADS + h:M_HEADS + h + 1]
        li_row = gr[h:h + 1, :]
        b_row = br_all[M_HEADS + h:M_HEADS + h + 1, :]
        g_tot = b_row[:, L - 1:L]
        m_prev = m_scr[h:h + 1, 0:1]
        qh = q_all[:, h * M_QK_DIM:(h + 1) * M_QK_DIM]
        kh = k_all[:, h * M_QK_DIM:(h + 1) * M_QK_DIM]
        vh = v_ref[:, h * M_V_DIM:(h + 1) * M_V_DIM]
        c_prev = c_scr[h]
        n_prev = n_scr[h:h + 1, :]

        dmat = jnp.where(causal, b_col - b_row + li_row, -jnp.inf)
        m_inter = b_col + m_prev
        m_t = jnp.maximum(m_inter, jnp.max(dmat, axis=1, keepdims=True))
        p = jnp.exp(dmat - m_t)
        s = lax.dot_general(qh, kh.astype(BF16), (((1,), (1,)), ((), ())),
                            preferred_element_type=F32) * p
        sc = jnp.exp(m_inter - m_t)
        num = (jnp.dot(s.astype(BF16), vh, preferred_element_type=F32)
               + sc * jnp.dot(qh, c_prev.astype(BF16), preferred_element_type=F32))
        den = (jnp.sum(s, axis=1, keepdims=True)
               + sc * jnp.sum(qh.astype(F32) * n_prev, axis=1, keepdims=True))
        inv = 1.0 / jnp.maximum(jnp.abs(den), jnp.exp(-m_t))

        a_col = g_tot - b_col + li_col
        m_new = jnp.maximum(g_tot + m_prev, jnp.max(a_col, axis=0, keepdims=True))
        kw = kh * jnp.exp(a_col - m_new)
        decay = jnp.exp(g_tot + m_prev - m_new)
        c_scr[h] = decay * c_prev + lax.dot_general(
            kw.astype(BF16), vh, (((0,), (0,)), ((), ())), preferred_element_type=F32)
        n_scr[h:h + 1, :] = decay * n_prev + jnp.sum(kw, axis=0, keepdims=True)
        m_scr[h:h + 1, :] = jnp.broadcast_to(m_new, (1, LANES))

        mu = jnp.mean(num, axis=1, keepdims=True)
        hc = num - mu
        var = jnp.mean(hc * hc, axis=1, keepdims=True)
        scale = inv * lax.rsqrt(inv * inv * var + EPS)
        cols = slice(h * M_V_DIM, (h + 1) * M_V_DIM)
        hn = hc * scale * (gain_ref[:, cols] * _sigmoid(o_ref[:, cols].astype(F32)))
        out_ref[:, cols] = hn.astype(out_ref.dtype)


def _mlstm(proj, gates_col, gates_row, conv_w, gain, batch, seq):
    L = MLSTM_CHUNK
    nc = seq // L
    row = lambda b, c: b * nc + c
    return pl.pallas_call(
        _mlstm_kernel,
        grid=(batch, nc),
        in_specs=[
            pl.BlockSpec((L, 2 * QK_WIDTH), lambda b, c: (row(b, c), 0)),
            pl.BlockSpec((L, M_WIDTH), lambda b, c: (row(b, c), COL_V // M_WIDTH)),
            pl.BlockSpec((L, M_WIDTH), lambda b, c: (row(b, c), COL_O // M_WIDTH)),
            pl.BlockSpec((L, LANES), lambda b, c: (row(b, c), 0)),
            pl.BlockSpec((2 * M_HEADS, L), lambda b, c: (0, row(b, c))),
            pl.BlockSpec((CONV_WIDTH, 2 * QK_WIDTH), lambda b, c: (0, 0)),
            pl.BlockSpec((1, M_WIDTH), lambda b, c: (0, 0)),
        ],
        out_specs=pl.BlockSpec((L, M_WIDTH), lambda b, c: (row(b, c), 0)),
        out_shape=jax.ShapeDtypeStruct((batch * seq, M_WIDTH), BF16),
        scratch_shapes=[
            pltpu.VMEM((M_HEADS, M_QK_DIM, M_V_DIM), F32),
            pltpu.VMEM((M_HEADS, M_QK_DIM), F32),
            pltpu.VMEM((M_HEADS, LANES), F32),
            pltpu.VMEM((SUBLANES, 2 * QK_WIDTH), F32),
        ],
        compiler_params=_params("parallel", "arbitrary"),
        name="mlstm",
    )(proj, proj, proj, gates_col, gates_row, conv_w, gain.reshape(1, M_WIDTH))


def _s5_kernel(u_ref, perm_ref, permt_ref, bbar_ref, cmat_ref, lam_ref, lamsub_ref, pow_ref,
               d_ref, wglu_ref, bglu_ref, out_ref, st_scr, x0_scr, bu_scr, xs_scr, y_scr):
    ns, nsub, w = S5_SUB, S5_NSUB, S5_TILE_STATES

    @pl.when(pl.program_id(1) == 0)
    def _():
        st_scr[...] = jnp.zeros_like(st_scr)

    up = jnp.dot(perm_ref[...], u_ref[...], preferred_element_type=F32).astype(BF16)

    for j in range(S5_TILES):
        ut = up[:, j * LANES:(j + 1) * LANES]
        bu_scr[...] = jnp.dot(ut, bbar_ref[j], preferred_element_type=F32)
        lr = lam_ref[0:1, j * w:(j + 1) * w]
        li = lam_ref[1:2, j * w:(j + 1) * w]

        def local_step(s, carry):
            xr, xi = carry
            rows = pl.ds(pl.multiple_of(s * nsub, nsub), nsub)
            nr = lr * xr - li * xi + bu_scr[rows, 0:w]
            ni = lr * xi + li * xr + bu_scr[rows, w:2 * w]
            bu_scr[rows, 0:w] = nr
            bu_scr[rows, w:2 * w] = ni
            return nr, ni

        zero = jnp.zeros((nsub, w), F32)
        er, ei = lax.fori_loop(0, ns, local_step, (zero, zero), unroll=4)

        x0_scr[0:nsub, :] = er
        x0_scr[nsub:2 * nsub, :] = ei
        lsr = lamsub_ref[0:1, j * w:(j + 1) * w]
        lsi = lamsub_ref[1:2, j * w:(j + 1) * w]

        def chain_step(c, carry):
            xr, xi = carry
            e_r = x0_scr[pl.ds(c, 1), :]
            e_i = x0_scr[pl.ds(nsub + c, 1), :]
            x0_scr[pl.ds(c, 1), :] = xr
            x0_scr[pl.ds(nsub + c, 1), :] = xi
            return lsr * xr - lsi * xi + e_r, lsr * xi + lsi * xr + e_i

        sr, si = lax.fori_loop(0, nsub, chain_step,
                               (st_scr[0:1, j * w:(j + 1) * w], st_scr[1:2, j * w:(j + 1) * w]),
                               unroll=4)
        st_scr[0:1, j * w:(j + 1) * w] = sr
        st_scr[1:2, j * w:(j + 1) * w] = si
        x0r = x0_scr[0:nsub, :]
        x0i = x0_scr[nsub:2 * nsub, :]

        def fix_step(s, carry):
            rows = pl.ds(pl.multiple_of(s * nsub, nsub), nsub)
            pr = pow_ref[pl.ds(s, 1), j * w:(j + 1) * w]
            pi = pow_ref[pl.ds(ns + s, 1), j * w:(j + 1) * w]
            xs_scr[rows, 0:w] = (bu_scr[rows, 0:w] + pr * x0r - pi * x0i).astype(BF16)
            xs_scr[rows, w:2 * w] = (bu_scr[rows, w:2 * w] + pr * x0i + pi * x0r).astype(BF16)
            return carry

        lax.fori_loop(0, ns, fix_step, 0, unroll=4)
        y_scr[:, j * LANES:(j + 1) * LANES] = (
            jnp.dot(xs_scr[...], cmat_ref[j], preferred_element_type=F32)
            + d_ref[:, j * LANES:(j + 1) * LANES] * ut.astype(F32))

    y = y_scr[...]
    z = 0.5 * y * (1.0 + jnp.tanh(math.sqrt(2.0 / math.pi) * (y + 0.044715 * (y * y * y))))
    gate = jnp.dot(z.astype(BF16), wglu_ref[...], preferred_element_type=F32) + bglu_ref[...]
    zg = (z * _sigmoid(gate)).astype(BF16)
    out_ref[...] = jnp.dot(permt_ref[...], zg, preferred_element_type=F32).astype(out_ref.dtype)


def _s5(proj, perm, permt, bbar, cmat, lam, lamsub, pows, d_skip, w_glu, b_glu, batch, seq):
    tb = S5_BLOCK
    nb = seq // tb
    full = lambda a: pl.BlockSpec(a.shape, lambda b, c: (0,) * a.ndim)
    return pl.pallas_call(
        _s5_kernel,
        grid=(batch, nb),
        in_specs=[pl.BlockSpec((tb, S5_WIDTH), lambda b, c: (b * nb + c, COL_U // S5_WIDTH)),
                  full(perm), full(permt), full(bbar), full(cmat), full(lam), full(lamsub),
                  full(pows), full(d_skip), full(w_glu), full(b_glu)],
        out_specs=pl.BlockSpec((tb, S5_WIDTH), lambda b, c: (b * nb + c, 0)),
        out_shape=jax.ShapeDtypeStruct((batch * seq, S5_WIDTH), BF16),
        scratch_shapes=[
            pltpu.VMEM((2, S5_STATES), F32),
            pltpu.VMEM((2 * S5_NSUB, S5_TILE_STATES), F32),
            pltpu.VMEM((tb, 2 * S5_TILE_STATES), F32),
            pltpu.VMEM((tb, 2 * S5_TILE_STATES), BF16),
            pltpu.VMEM((tb, S5_WIDTH), F32),
        ],
        compiler_params=_params("parallel", "arbitrary"),
        name="s5",
    )(proj, perm, permt, bbar, cmat, lam, lamsub, pows, d_skip, w_glu, b_glu)


def _s5_tables(a_re, a_im, log_step, b_re, b_im, c_re, c_im):
    lam_re = jnp.minimum(a_re, -1e-4)
    lam_im = a_im
    dt = jnp.exp(log_step)[:, None]
    mag = jnp.exp(lam_re * dt)
    bar_re = mag * jnp.cos(lam_im * dt)
    bar_im = mag * jnp.sin(lam_im * dt)
    den = lam_re * lam_re + lam_im * lam_im
    nr = bar_re - 1.0
    r_re = (nr * lam_re + bar_im * lam_im) / den
    r_im = (bar_im * lam_re - nr * lam_im) / den
    bb_re = r_re[..., None] * b_re - r_im[..., None] * b_im
    bb_im = r_re[..., None] * b_im + r_im[..., None] * b_re

    gpt = S5_GROUPS // S5_TILES
    eye = jnp.eye(gpt, dtype=F32)

    def in_tile(bb):
        t = bb.reshape(S5_TILES, gpt, S5_STATE, S5_GROUP)
        t = jnp.einsum('jgpn,gh->jgnhp', t, eye)
        return t.reshape(S5_TILES, LANES, S5_TILE_STATES)

    def out_tile(c):
        t = c.reshape(S5_TILES, gpt, S5_GROUP, S5_STATE)
        t = jnp.einsum('jgnp,gh->jgphn', t, eye)
        return t.reshape(S5_TILES, S5_TILE_STATES, LANES)

    bbar = jnp.concatenate([in_tile(bb_re), in_tile(bb_im)], axis=2).astype(BF16)
    cmat = jnp.concatenate([out_tile(c_re), -out_tile(c_im)], axis=1).astype(BF16)
    lam = jnp.stack([bar_re.reshape(-1), bar_im.reshape(-1)])

    def power(k):
        mk = jnp.exp(k * lam_re * dt)
        return (mk * jnp.cos(k * lam_im * dt)).reshape(-1), (mk * jnp.sin(k * lam_im * dt)).reshape(-1)

    lamsub = jnp.stack(power(float(S5_SUB)))
    steps = [power(float(s + 1)) for s in range(S5_SUB)]
    pows = jnp.stack([p[0] for p in steps] + [p[1] for p in steps])
    return bbar, cmat, lam, lamsub, pows


def _s5_perm():
    r = jnp.arange(S5_BLOCK)
    src = (r % S5_NSUB) * S5_SUB + r // S5_NSUB
    perm = (src[:, None] == jnp.arange(S5_BLOCK)[None, :]).astype(BF16)
    return perm, perm.T


def _merge_kernel(ha_ref, zb_ref, pa_ref, pb_ref, ga_ref, gb_ref, o_ref, pab_scr, pbb_scr):
    @pl.when(pl.program_id(1) == 0)
    def _():
        pab_scr[...] = pa_ref[...].astype(BF16)
        pbb_scr[...] = pb_ref[...].astype(BF16)

    ya = jnp.dot(ha_ref[...], pab_scr[...], preferred_element_type=F32)
    yb = jnp.dot(zb_ref[...], pbb_scr[...], preferred_element_type=F32)
    sa = _sigmoid(ga_ref[...].astype(F32))
    sb = _sigmoid(gb_ref[...].astype(F32))
    o_ref[...] = (sa * ya + sb * yb).astype(o_ref.dtype)


def _merge(ha, zb, pa_stack, pb_stack, layer, proj, tm=512, tn=1024):
    t = ha.shape[0]
    return pl.pallas_call(
        _merge_kernel,
        grid=(D_MODEL // tn, t // tm),
        in_specs=[pl.BlockSpec((tm, M_WIDTH), lambda j, i: (i, 0)),
                  pl.BlockSpec((tm, S5_WIDTH), lambda j, i: (i, 0)),
                  pl.BlockSpec((None, M_WIDTH, tn), lambda j, i: (layer, 0, j)),
                  pl.BlockSpec((None, S5_WIDTH, tn), lambda j, i: (layer, 0, j)),
                  pl.BlockSpec((tm, tn), lambda j, i: (i, COL_GA // tn + j)),
                  pl.BlockSpec((tm, tn), lambda j, i: (i, COL_GB // tn + j))],
        out_specs=pl.BlockSpec((tm, tn), lambda j, i: (i, j)),
        out_shape=jax.ShapeDtypeStruct((t, D_MODEL), BF16),
        scratch_shapes=[pltpu.VMEM((M_WIDTH, tn), BF16), pltpu.VMEM((S5_WIDTH, tn), BF16)],
        compiler_params=_params("parallel", "arbitrary"),
        name="merge",
    )(ha, zb, pa_stack, pb_stack, proj, proj)


def _split_w_in(w):
    o1 = 2 * QK_WIDTH
    o2 = o1 + M_WIDTH
    o3 = o2 + M_WIDTH
    o5 = o3 + 2 * M_HEADS
    main = jnp.concatenate([w[:, :o3], w[:, o5:]], axis=1).astype(BF16)
    wg = w[:, o3:o5]
    wg_col = jnp.pad(wg, ((0, 0), (0, LANES - 2 * M_HEADS))).astype(BF16)
    wg_row = wg.T.astype(BF16)
    return main, wg_col, wg_row


def kernel(x, pre_mix_gain, post_mix_gain, pre_mlp_gain, post_mlp_gain, w_in, conv_qk, b_igate, b_fgate, mh_norm_gain, p_a, s5_a_re, s5_a_im, s5_log_step, s5_b_re, s5_b_im, s5_c_re, s5_c_im, s5_d, w_glu, b_glu, p_b, w_out, w_ff1, w_ff2):
    batch, seq, d = x.shape
    depth = w_in.shape[0]
    t = batch * seq
    xf = x.reshape(t, d)
    perm, permt = _s5_perm()
    xn = _rmsnorm(xf, pre_mix_gain[0])
    for l in range(depth):
        w_main, wg_col, wg_row = _split_w_in(w_in[l])
        bias = jnp.concatenate([b_igate[l], b_fgate[l]])
        bcol = jnp.pad(bias, (0, LANES - 2 * M_HEADS)).reshape(1, LANES)
        brow = bias.reshape(2 * M_HEADS, 1)

        proj, gates_col, gates_row = _proj(xn, w_main, wg_col, wg_row, bcol, brow)
        ha = _mlstm(proj, gates_col, gates_row, conv_qk[l], mh_norm_gain[l], batch, seq)
        tables = _s5_tables(s5_a_re[l], s5_a_im[l], s5_log_step[l], s5_b_re[l], s5_b_im[l],
                            s5_c_re[l], s5_c_im[l])
        zb = _s5(proj, perm, permt, *tables, s5_d[l].reshape(1, S5_WIDTH), w_glu[l].astype(BF16),
                 b_glu[l].reshape(1, S5_WIDTH), batch, seq)
        merged = _merge(ha, zb, p_a, p_b, l, proj)
        mix = _matmul_wcast(merged, w_out, l, "w_out")
        xf, hn = _norm_residual(mix, xf, post_mix_gain[l], pre_mlp_gain[l])
        hid, w2b = _matmul_wcast(hn, w_ff1, l, "ff1", relu2=True, ride_stack=w_ff2)
        ff = _matmul(hid, w2b, BF16, tm=512, tn=256, name="ff2")
        g_next = pre_mix_gain[l + 1] if l + 1 < depth else None
        xf, xn = _norm_residual(ff, xf, post_mlp_gain[l], g_next)
    return xf.reshape(batch, seq, d)
```

```python
import functools
import math

import jax
import jax.numpy as jnp
from jax import lax
from jax.experimental import pallas as pl
from jax.experimental.pallas import tpu as pltpu

F32 = jnp.float32
BF16 = jnp.bfloat16

D_MODEL = 4096
M_HEADS = 8
M_V_DIM = D_MODEL // (2 * M_HEADS)
M_QK_DIM = M_V_DIM // 2
M_WIDTH = M_HEADS * M_V_DIM
QK_WIDTH = M_HEADS * M_QK_DIM
CONV_WIDTH = 4
S5_WIDTH = D_MODEL // 4
S5_GROUP = 16
S5_GROUPS = S5_WIDTH // S5_GROUP
S5_STATE = 64
S5_STATES = S5_GROUPS * S5_STATE
D_FF = 4 * D_MODEL
EPS = 1e-6

LANES = 128
SUBLANES = 8
VMEM_LIMIT = 56 * 1024 * 1024

PROJ_MAIN = 2 * QK_WIDTH + 2 * M_WIDTH + S5_WIDTH + 2 * D_MODEL
COL_V = 2 * QK_WIDTH
COL_O = COL_V + M_WIDTH
COL_U = COL_O + M_WIDTH
COL_GA = COL_U + S5_WIDTH
COL_GB = COL_GA + D_MODEL

MLSTM_CHUNK = 256
S5_BLOCK = 512
S5_SUB = 32
S5_NSUB = S5_BLOCK // S5_SUB
S5_TILE_STATES = 512
S5_TILES = S5_WIDTH // LANES


def _params(*sem):
    return pltpu.CompilerParams(dimension_semantics=sem, vmem_limit_bytes=VMEM_LIMIT)


def _sigmoid(x):
    return 1.0 / (1.0 + jnp.exp(-x))


def _rmsnorm_kernel(x_ref, g_ref, o_ref):
    x = x_ref[...]
    ms = jnp.mean(x * x, axis=-1, keepdims=True)
    o_ref[...] = (x * lax.rsqrt(ms + EPS) * g_ref[...]).astype(o_ref.dtype)


def _rmsnorm(x, gain, tm=256):
    t, d = x.shape
    return pl.pallas_call(
        _rmsnorm_kernel,
        grid=(t // tm,),
        in_specs=[pl.BlockSpec((tm, d), lambda i: (i, 0)),
                  pl.BlockSpec((1, d), lambda i: (0, 0))],
        out_specs=pl.BlockSpec((tm, d), lambda i: (i, 0)),
        out_shape=jax.ShapeDtypeStruct((t, d), BF16),
        compiler_params=_params("parallel"),
        name="rmsnorm",
    )(x, gain.reshape(1, d))


def _norm_residual_kernel(y_ref, x_ref, gpost_ref, gnext_ref, xo_ref, hn_ref):
    y = y_ref[...].astype(F32)
    ms = jnp.mean(y * y, axis=-1, keepdims=True)
    xn = x_ref[...] + y * lax.rsqrt(ms + EPS) * gpost_ref[...]
    xo_ref[...] = xn
    if hn_ref is not None:
        ms2 = jnp.mean(xn * xn, axis=-1, keepdims=True)
        hn_ref[...] = (xn * lax.rsqrt(ms2 + EPS) * gnext_ref[...]).astype(hn_ref.dtype)


def _norm_residual_last_kernel(y_ref, x_ref, gpost_ref, xo_ref):
    _norm_residual_kernel(y_ref, x_ref, gpost_ref, None, xo_ref, None)


def _norm_residual(y, x, g_post, g_next, tm=256):
    t, d = x.shape
    row = pl.BlockSpec((tm, d), lambda i: (i, 0))
    vec = pl.BlockSpec((1, d), lambda i: (0, 0))
    if g_next is None:
        xo = pl.pallas_call(
            _norm_residual_last_kernel, grid=(t // tm,),
            in_specs=[row, row, vec], out_specs=row,
            out_shape=jax.ShapeDtypeStruct((t, d), F32),
            compiler_params=_params("parallel"),
            name="norm_residual_last",
        )(y, x, g_post.reshape(1, d))
        return xo, None
    return pl.pallas_call(
        _norm_residual_kernel, grid=(t // tm,),
        in_specs=[row, row, vec, vec], out_specs=[row, row],
        out_shape=[jax.ShapeDtypeStruct((t, d), F32), jax.ShapeDtypeStruct((t, d), BF16)],
        compiler_params=_params("parallel"),
        name="norm_residual",
    )(y, x, g_post.reshape(1, d), g_next.reshape(1, d))


def _log_sigmoid(z):
    return jnp.minimum(z, 0.0) - jnp.log1p(jnp.exp(-jnp.abs(z)))


def _cast_kernel(w_ref, o_ref):
    o_ref[...] = w_ref[...].astype(o_ref.dtype)


def _cast(w_stack, layer, rb=512, cb=2048):
    r, c = w_stack.shape[1:]
    return pl.pallas_call(
        _cast_kernel,
        grid=(r // rb, c // cb),
        in_specs=[pl.BlockSpec((None, rb, cb), lambda i, j: (layer, i, j))],
        out_specs=pl.BlockSpec((rb, cb), lambda i, j: (i, j)),
        out_shape=jax.ShapeDtypeStruct((r, c), BF16),
        compiler_params=_params("parallel", "parallel"),
        name="cast",
    )(w_stack)


W_IN_TILE = 512
W_IN_HEAD_TILES = COL_U // W_IN_TILE
W_IN_SKIP = 2 * M_HEADS


def _cast_w_in_kernel(a_ref, b_ref, o_ref):
    tn = o_ref.shape[1]

    @pl.when(pl.program_id(1) < W_IN_HEAD_TILES)
    def _():
        o_ref[...] = a_ref[...].astype(o_ref.dtype)

    @pl.when(pl.program_id(1) >= W_IN_HEAD_TILES)
    def _():
        full = jnp.concatenate([a_ref[...], b_ref[...]], axis=1)
        shifted = pltpu.roll(full, tn + LANES - W_IN_SKIP, 1)
        o_ref[...] = shifted[:, :tn].astype(o_ref.dtype)


def _cast_w_in(w_in, layer, rb=512):
    d = w_in.shape[1]
    tn = W_IN_TILE
    per = tn // LANES
    return pl.pallas_call(
        _cast_w_in_kernel,
        grid=(d // rb, PROJ_MAIN // tn),
        in_specs=[pl.BlockSpec((None, rb, tn), lambda i, j: (layer, i, j)),
                  pl.BlockSpec((None, rb, LANES), lambda i, j: (layer, i, per * (j + 1)))],
        out_specs=pl.BlockSpec((rb, tn), lambda i, j: (i, j)),
        out_shape=jax.ShapeDtypeStruct((d, PROJ_MAIN), BF16),
        compiler_params=_params("parallel", "parallel"),
        name="cast_w_in",
    )(w_in, w_in)


FF2_TILE = 256


def _matmul_kernel(a_ref, w_ref, *rest, relu2, ride):
    if ride:
        w2_ref, o_ref, w2b_ref = rest
        for jt in range(w2b_ref.shape[0]):
            w2b_ref[jt] = w2_ref[:, jt * FF2_TILE:(jt + 1) * FF2_TILE].astype(BF16)
    else:
        (o_ref,) = rest
    acc = jnp.dot(a_ref[...], w_ref[...], preferred_element_type=F32)
    if relu2:
        acc = jnp.square(jnp.maximum(acc, 0.0))
    o_ref[...] = acc.astype(o_ref.dtype)


def _matmul(a, w, tm, tn, name, relu2=False, ride_stack=None, layer=0):
    m, k = a.shape
    if w.ndim == 3:
        n = w.shape[0] * tn
        w_spec = pl.BlockSpec((None, k, tn), lambda i, j: (j, 0, 0))
    else:
        n = w.shape[1]
        w_spec = pl.BlockSpec((k, tn), lambda i, j: (0, j))
    ni, nj = m // tm, n // tn
    in_specs = [pl.BlockSpec((tm, k), lambda i, j: (i, 0)), w_spec]
    out_specs = [pl.BlockSpec((tm, tn), lambda i, j: (i, j))]
    out_shape = [jax.ShapeDtypeStruct((m, n), BF16)]
    args = [a, w]
    if ride_stack is not None:
        r, c = ride_stack.shape[1:]
        rb = r // (ni * nj)
        assert rb * ni * nj == r and rb % (2 * SUBLANES) == 0
        nt = c // FF2_TILE
        in_specs.append(pl.BlockSpec((None, rb, c), lambda i, j: (layer, i * nj + j, 0)))
        out_specs.append(pl.BlockSpec((nt, rb, FF2_TILE), lambda i, j: (0, i * nj + j, 0)))
        out_shape.append(jax.ShapeDtypeStruct((nt, r, FF2_TILE), BF16))
        args.append(ride_stack)
    out = pl.pallas_call(
        functools.partial(_matmul_kernel, relu2=relu2, ride=ride_stack is not None),
        grid=(ni, nj),
        in_specs=in_specs, out_specs=out_specs, out_shape=out_shape,
        compiler_params=_params("parallel", "parallel"),
        name=name,
    )(*args)
    return out if ride_stack is not None else out[0]


def _matmul_wcast_kernel(a_ref, w_ref, o_ref, wb_scr):
    @pl.when(pl.program_id(1) == 0)
    def _():
        wb_scr[...] = w_ref[...].astype(BF16)

    o_ref[...] = jnp.dot(a_ref[...], wb_scr[...], preferred_element_type=F32).astype(o_ref.dtype)


def _matmul_wcast(a, w_stack, layer, name, tm=1024, tn=512):
    m, k = a.shape
    n = w_stack.shape[2]
    return pl.pallas_call(
        _matmul_wcast_kernel,
        grid=(n // tn, m // tm),
        in_specs=[pl.BlockSpec((tm, k), lambda j, i: (i, 0)),
                  pl.BlockSpec((None, k, tn), lambda j, i: (layer, 0, j))],
        out_specs=pl.BlockSpec((tm, tn), lambda j, i: (i, j)),
        out_shape=jax.ShapeDtypeStruct((m, n), BF16),
        scratch_shapes=[pltpu.VMEM((k, tn), BF16)],
        compiler_params=_params("parallel", "arbitrary"),
        name=name,
    )(a, w_stack)


def _cumsum_rows(x):
    n = x.shape[0]
    row = lax.broadcasted_iota(jnp.int32, x.shape, 0)
    k = 1
    while k < n:
        x = x + jnp.where(row >= k, pltpu.roll(x, k, 0), 0.0)
        k *= 2
    return x


def _cummax_rows(x):
    n = x.shape[0]
    row = lax.broadcasted_iota(jnp.int32, x.shape, 0)
    k = 1
    while k < n:
        x = jnp.maximum(x, jnp.where(row >= k, pltpu.roll(x, k, 0), -jnp.inf))
        k *= 2
    return x


def _mlstm_kernel(xn_ref, wg_ref, bias_ref, qk_ref, v_ref, o_ref, cw_ref, gain_ref, out_ref,
                  wgb_scr, c_scr, n_scr, m_scr, tail_scr):
    L = qk_ref.shape[0]

    @pl.when(pl.program_id(1) == 0)
    def _():
        wgb_scr[...] = wg_ref[...].astype(BF16)
        c_scr[...] = jnp.zeros_like(c_scr)
        n_scr[...] = jnp.zeros_like(n_scr)
        m_scr[...] = jnp.zeros_like(m_scr)
        tail_scr[...] = jnp.zeros_like(tail_scr)

    x = qk_ref[...].astype(F32)
    tail = tail_scr[...]
    row8 = lax.broadcasted_iota(jnp.int32, tail.shape, 0)
    acc = x * cw_ref[CONV_WIDTH - 1:CONV_WIDTH, :]
    for sh in range(1, CONV_WIDTH):
        r = pltpu.roll(x, sh, 0)
        head = jnp.where(row8 < sh, pltpu.roll(tail, sh, 0), r[0:SUBLANES])
        r = jnp.concatenate([head, r[SUBLANES:]], axis=0)
        acc = acc + r * cw_ref[CONV_WIDTH - 1 - sh:CONV_WIDTH - sh, :]
    tail_scr[...] = x[L - SUBLANES:L]
    qk = acc * _sigmoid(acc)
    q_all = (qk[:, :QK_WIDTH] * (M_QK_DIM ** -0.5)).astype(BF16)
    k_all = qk[:, QK_WIDTH:]

    z = jnp.dot(xn_ref[...], wgb_scr[...], preferred_element_type=F32) + bias_ref[...]
    lane = lax.broadcasted_iota(jnp.int32, z.shape, 1)
    used = lane < M_HEADS
    log_i = jnp.where(used, z, 0.0)
    log_f = jnp.where(used, pltpu.roll(_log_sigmoid(z), LANES - M_HEADS, 1), 0.0)
    b = _cumsum_rows(log_f)
    r = log_i - b
    m_prev = m_scr[0:1, :]
    m_t = b + jnp.maximum(m_prev, _cummax_rows(r))
    col_all = b - m_t
    sc_all = jnp.exp(b + m_prev - m_t)
    eneg_all = jnp.exp(-m_t)
    g_tot = b[L - 1:L, :]
    a_all = g_tot + r
    m_new = jnp.maximum(g_tot + m_prev, jnp.max(a_all, axis=0, keepdims=True))
    wk_all = jnp.exp(a_all - m_new)
    decay_all = jnp.exp(g_tot + m_prev - m_new)
    m_scr[0:1, :] = m_new
    r_rows = r.T

    ti = lax.broadcasted_iota(jnp.int32, (L, L), 0)
    si = lax.broadcasted_iota(jnp.int32, (L, L), 1)
    causal = ti >= si

    for h in range(M_HEADS):
        qh = q_all[:, h * M_QK_DIM:(h + 1) * M_QK_DIM]
        kh = k_all[:, h * M_QK_DIM:(h + 1) * M_QK_DIM]
        vh = v_ref[:, h * M_V_DIM:(h + 1) * M_V_DIM]
        c_prev = c_scr[h]
        n_prev = n_scr[h:h + 1, :]
        sc = sc_all[:, h:h + 1]

        p = jnp.exp(jnp.where(causal, col_all[:, h:h + 1] + r_rows[h:h + 1, :], -jnp.inf))
        s = lax.dot_general(qh, kh.astype(BF16), (((1,), (1,)), ((), ())),
                            preferred_element_type=F32) * p
        num = (jnp.dot(s.astype(BF16), vh, preferred_element_type=F32)
               + sc * jnp.dot(qh, c_prev.astype(BF16), preferred_element_type=F32))
        den = (jnp.sum(s, axis=1, keepdims=True)
               + sc * jnp.sum(qh.astype(F32) * n_prev, axis=1, keepdims=True))
        inv = 1.0 / jnp.maximum(jnp.abs(den), eneg_all[:, h:h + 1])

        kw = kh * wk_all[:, h:h + 1]
        decay = decay_all[:, h:h + 1]
        c_scr[h] = decay * c_prev + lax.dot_general(
            kw.astype(BF16), vh, (((0,), (0,)), ((), ())), preferred_element_type=F32)
        n_scr[h:h + 1, :] = decay * n_prev + jnp.sum(kw, axis=0, keepdims=True)

        mu = jnp.mean(num, axis=1, keepdims=True)
        hc = num - mu
        var = jnp.mean(hc * hc, axis=1, keepdims=True)
        scale = inv * lax.rsqrt(inv * inv * var + EPS)
        cols = slice(h * M_V_DIM, (h + 1) * M_V_DIM)
        hn = hc * scale * (gain_ref[:, cols] * _sigmoid(o_ref[:, cols].astype(F32)))
        out_ref[:, cols] = hn.astype(out_ref.dtype)


def _mlstm(xn, w_in, layer, bias, proj, conv_w, gain, batch, seq):
    L = MLSTM_CHUNK
    nc = seq // L
    d = xn.shape[1]
    row = lambda b, c: b * nc + c
    return pl.pallas_call(
        _mlstm_kernel,
        grid=(batch, nc),
        in_specs=[
            pl.BlockSpec((L, d), lambda b, c: (row(b, c), 0)),
            pl.BlockSpec((None, d, LANES), lambda b, c: (layer, 0, COL_U // LANES)),
            pl.BlockSpec((1, LANES), lambda b, c: (0, 0)),
            pl.BlockSpec((L, 2 * QK_WIDTH), lambda b, c: (row(b, c), 0)),
            pl.BlockSpec((L, M_WIDTH), lambda b, c: (row(b, c), COL_V // M_WIDTH)),
            pl.BlockSpec((L, M_WIDTH), lambda b, c: (row(b, c), COL_O // M_WIDTH)),
            pl.BlockSpec((CONV_WIDTH, 2 * QK_WIDTH), lambda b, c: (0, 0)),
            pl.BlockSpec((1, M_WIDTH), lambda b, c: (0, 0)),
        ],
        out_specs=pl.BlockSpec((L, M_WIDTH), lambda b, c: (row(b, c), 0)),
        out_shape=jax.ShapeDtypeStruct((batch * seq, M_WIDTH), BF16),
        scratch_shapes=[
            pltpu.VMEM((d, LANES), BF16),
            pltpu.VMEM((M_HEADS, M_QK_DIM, M_V_DIM), F32),
            pltpu.VMEM((M_HEADS, M_QK_DIM), F32),
            pltpu.VMEM((SUBLANES, LANES), F32),
            pltpu.VMEM((SUBLANES, 2 * QK_WIDTH), F32),
        ],
        compiler_params=_params("parallel", "arbitrary"),
        name="mlstm",
    )(xn, w_in, bias, proj, proj, proj, conv_w, gain.reshape(1, M_WIDTH))


def _s5_kernel(u_ref, perm_ref, permt_ref, bbar_ref, cmat_ref, lam_ref, lamsub_ref, pow_ref,
               d_ref, wglu_ref, bglu_ref, out_ref, st_scr, x0_scr, bu_scr, xs_scr, y_scr):
    ns, nsub, w = S5_SUB, S5_NSUB, S5_TILE_STATES

    @pl.when(pl.program_id(1) == 0)
    def _():
        st_scr[...] = jnp.zeros_like(st_scr)

    up = jnp.dot(perm_ref[...], u_ref[...], preferred_element_type=F32).astype(BF16)

    for j in range(S5_TILES):
        ut = up[:, j * LANES:(j + 1) * LANES]
        bu_scr[...] = jnp.dot(ut, bbar_ref[j], preferred_element_type=F32)
        lr = lam_ref[0:1, j * w:(j + 1) * w]
        li = lam_ref[1:2, j * w:(j + 1) * w]

        def local_step(s, carry):
            xr, xi = carry
            rows = pl.ds(pl.multiple_of(s * nsub, nsub), nsub)
            nr = lr * xr - li * xi + bu_scr[rows, 0:w]
            ni = lr * xi + li * xr + bu_scr[rows, w:2 * w]
            bu_scr[rows, 0:w] = nr
            bu_scr[rows, w:2 * w] = ni
            return nr, ni

        zero = jnp.zeros((nsub, w), F32)
        er, ei = lax.fori_loop(0, ns, local_step, (zero, zero), unroll=True)

        x0_scr[0:nsub, :] = er
        x0_scr[nsub:2 * nsub, :] = ei
        lsr = lamsub_ref[0:1, j * w:(j + 1) * w]
        lsi = lamsub_ref[1:2, j * w:(j + 1) * w]

        def chain_step(c, carry):
            xr, xi = carry
            e_r = x0_scr[pl.ds(c, 1), :]
            e_i = x0_scr[pl.ds(nsub + c, 1), :]
            x0_scr[pl.ds(c, 1), :] = xr
            x0_scr[pl.ds(nsub + c, 1), :] = xi
            return lsr * xr - lsi * xi + e_r, lsr * xi + lsi * xr + e_i

        sr, si = lax.fori_loop(0, nsub, chain_step,
                               (st_scr[0:1, j * w:(j + 1) * w], st_scr[1:2, j * w:(j + 1) * w]),
                               unroll=True)
        st_scr[0:1, j * w:(j + 1) * w] = sr
        st_scr[1:2, j * w:(j + 1) * w] = si
        x0r = x0_scr[0:nsub, :]
        x0i = x0_scr[nsub:2 * nsub, :]

        def fix_step(s, carry):
            rows = pl.ds(pl.multiple_of(s * nsub, nsub), nsub)
            pr = pow_ref[pl.ds(s, 1), j * w:(j + 1) * w]
            pi = pow_ref[pl.ds(ns + s, 1), j * w:(j + 1) * w]
            xs_scr[rows, 0:w] = (bu_scr[rows, 0:w] + pr * x0r - pi * x0i).astype(BF16)
            xs_scr[rows, w:2 * w] = (bu_scr[rows, w:2 * w] + pr * x0i + pi * x0r).astype(BF16)
            return carry

        lax.fori_loop(0, ns, fix_step, 0, unroll=True)
        y_scr[:, j * LANES:(j + 1) * LANES] = (
            jnp.dot(xs_scr[...], cmat_ref[j], preferred_element_type=F32)
            + d_ref[:, j * LANES:(j + 1) * LANES] * ut.astype(F32))

    y = y_scr[...]
    z = 0.5 * y * (1.0 + jnp.tanh(math.sqrt(2.0 / math.pi) * (y + 0.044715 * (y * y * y))))
    gate = jnp.dot(z.astype(BF16), wglu_ref[...], preferred_element_type=F32) + bglu_ref[...]
    zg = (z * _sigmoid(gate)).astype(BF16)
    out_ref[...] = jnp.dot(permt_ref[...], zg, preferred_element_type=F32).astype(out_ref.dtype)


def _s5(proj, perm, permt, bbar, cmat, lam, lamsub, pows, d_skip, w_glu, b_glu, batch, seq):
    tb = S5_BLOCK
    nb = seq // tb
    full = lambda a: pl.BlockSpec(a.shape, lambda b, c: (0,) * a.ndim)
    return pl.pallas_call(
        _s5_kernel,
        grid=(batch, nb),
        in_specs=[pl.BlockSpec((tb, S5_WIDTH), lambda b, c: (b * nb + c, COL_U // S5_WIDTH)),
                  full(perm), full(permt), full(bbar), full(cmat), full(lam), full(lamsub),
                  full(pows), full(d_skip), full(w_glu), full(b_glu)],
        out_specs=pl.BlockSpec((tb, S5_WIDTH), lambda b, c: (b * nb + c, 0)),
        out_shape=jax.ShapeDtypeStruct((batch * seq, S5_WIDTH), BF16),
        scratch_shapes=[
            pltpu.VMEM((2, S5_STATES), F32),
            pltpu.VMEM((2 * S5_NSUB, S5_TILE_STATES), F32),
            pltpu.VMEM((tb, 2 * S5_TILE_STATES), F32),
            pltpu.VMEM((tb, 2 * S5_TILE_STATES), BF16),
            pltpu.VMEM((tb, S5_WIDTH), F32),
        ],
        compiler_params=_params("parallel", "arbitrary"),
        name="s5",
    )(proj, perm, permt, bbar, cmat, lam, lamsub, pows, d_skip, w_glu, b_glu)


def _s5_tables(a_re, a_im, log_step, b_re, b_im, c_re, c_im):
    lam_re = jnp.minimum(a_re, -1e-4)
    lam_im = a_im
    dt = jnp.exp(log_step)[:, None]
    mag = jnp.exp(lam_re * dt)
    bar_re = mag * jnp.cos(lam_im * dt)
    bar_im = mag * jnp.sin(lam_im * dt)
    den = lam_re * lam_re + lam_im * lam_im
    nr = bar_re - 1.0
    r_re = (nr * lam_re + bar_im * lam_im) / den
    r_im = (bar_im * lam_re - nr * lam_im) / den
    bb_re = r_re[..., None] * b_re - r_im[..., None] * b_im
    bb_im = r_re[..., None] * b_im + r_im[..., None] * b_re

    gpt = S5_GROUPS // S5_TILES
    eye = jnp.eye(gpt, dtype=F32)

    def in_tile(bb):
        t = bb.reshape(S5_TILES, gpt, S5_STATE, S5_GROUP)
        t = jnp.einsum('jgpn,gh->jgnhp', t, eye)
        return t.reshape(S5_TILES, LANES, S5_TILE_STATES)

    def out_tile(c):
        t = c.reshape(S5_TILES, gpt, S5_GROUP, S5_STATE)
        t = jnp.einsum('jgnp,gh->jgphn', t, eye)
        return t.reshape(S5_TILES, S5_TILE_STATES, LANES)

    bbar = jnp.concatenate([in_tile(bb_re), in_tile(bb_im)], axis=2).astype(BF16)
    cmat = jnp.concatenate([out_tile(c_re), -out_tile(c_im)], axis=1).astype(BF16)
    lam = jnp.stack([bar_re.reshape(-1), bar_im.reshape(-1)])

    def power(k):
        mk = jnp.exp(k * lam_re * dt)
        return (mk * jnp.cos(k * lam_im * dt)).reshape(-1), (mk * jnp.sin(k * lam_im * dt)).reshape(-1)

    lamsub = jnp.stack(power(float(S5_SUB)))
    steps = [power(float(s + 1)) for s in range(S5_SUB)]
    pows = jnp.stack([p[0] for p in steps] + [p[1] for p in steps])
    return bbar, cmat, lam, lamsub, pows


def _s5_perm():
    r = jnp.arange(S5_BLOCK)
    src = (r % S5_NSUB) * S5_SUB + r // S5_NSUB
    perm = (src[:, None] == jnp.arange(S5_BLOCK)[None, :]).astype(BF16)
    return perm, perm.T


def _merge_kernel(ha_ref, zb_ref, pa_ref, pb_ref, ga_ref, gb_ref, o_ref, pab_scr, pbb_scr):
    @pl.when(pl.program_id(1) == 0)
    def _():
        pab_scr[...] = pa_ref[...].astype(BF16)
        pbb_scr[...] = pb_ref[...].astype(BF16)

    ya = jnp.dot(ha_ref[...], pab_scr[...], preferred_element_type=F32)
    yb = jnp.dot(zb_ref[...], pbb_scr[...], preferred_element_type=F32)
    sa = _sigmoid(ga_ref[...].astype(F32))
    sb = _sigmoid(gb_ref[...].astype(F32))
    o_ref[...] = (sa * ya + sb * yb).astype(o_ref.dtype)


def _merge(ha, zb, pa_stack, pb_stack, layer, proj, tm=512, tn=1024):
    t = ha.shape[0]
    return pl.pallas_call(
        _merge_kernel,
        grid=(D_MODEL // tn, t // tm),
        in_specs=[pl.BlockSpec((tm, M_WIDTH), lambda j, i: (i, 0)),
                  pl.BlockSpec((tm, S5_WIDTH), lambda j, i: (i, 0)),
                  pl.BlockSpec((None, M_WIDTH, tn), lambda j, i: (layer, 0, j)),
                  pl.BlockSpec((None, S5_WIDTH, tn), lambda j, i: (layer, 0, j)),
                  pl.BlockSpec((tm, tn), lambda j, i: (i, COL_GA // tn + j)),
                  pl.BlockSpec((tm, tn), lambda j, i: (i, COL_GB // tn + j))],
        out_specs=pl.BlockSpec((tm, tn), lambda j, i: (i, j)),
        out_shape=jax.ShapeDtypeStruct((t, D_MODEL), BF16),
        scratch_shapes=[pltpu.VMEM((M_WIDTH, tn), BF16), pltpu.VMEM((S5_WIDTH, tn), BF16)],
        compiler_params=_params("parallel", "arbitrary"),
        name="merge",
    )(ha, zb, pa_stack, pb_stack, proj, proj)


def kernel(x, pre_mix_gain, post_mix_gain, pre_mlp_gain, post_mlp_gain, w_in, conv_qk, b_igate, b_fgate, mh_norm_gain, p_a, s5_a_re, s5_a_im, s5_log_step, s5_b_re, s5_b_im, s5_c_re, s5_c_im, s5_d, w_glu, b_glu, p_b, w_out, w_ff1, w_ff2):
    batch, seq, d = x.shape
    depth = w_in.shape[0]
    t = batch * seq
    xf = x.reshape(t, d)
    perm, permt = _s5_perm()
    xn = _rmsnorm(xf, pre_mix_gain[0])
    for l in range(depth):
        bias = jnp.concatenate([b_igate[l], b_fgate[l]])
        bias = jnp.pad(bias, (0, LANES - 2 * M_HEADS)).reshape(1, LANES)

        proj = _matmul(xn, _cast_w_in(w_in, l), 1024, 1024, "proj")
        ha = _mlstm(xn, w_in, l, bias, proj, conv_qk[l], mh_norm_gain[l], batch, seq)
        tables = _s5_tables(s5_a_re[l], s5_a_im[l], s5_log_step[l], s5_b_re[l], s5_b_im[l],
                            s5_c_re[l], s5_c_im[l])
        zb = _s5(proj, perm, permt, *tables, s5_d[l].reshape(1, S5_WIDTH), w_glu[l].astype(BF16),
                 b_glu[l].reshape(1, S5_WIDTH), batch, seq)
        merged = _merge(ha, zb, p_a, p_b, l, proj)
        mix = _matmul_wcast(merged, w_out, l, "w_out")
        xf, hn = _norm_residual(mix, xf, post_mix_gain[l], pre_mlp_gain[l])
        hid, w2b = _matmul(hn, _cast(w_ff1, l), 1024, 1024, "ff1", relu2=True, ride_stack=w_ff2, layer=l)
        ff = _matmul(hid, w2b, 512, FF2_TILE, "ff2")
        g_next = pre_mix_gain[l + 1] if l + 1 < depth else None
        xf, xn = _norm_residual(ff, xf, post_mlp_gain[l], g_next)
    return xf.reshape(batch, seq, d)
```

```python
import functools
import math

import jax
import jax.numpy as jnp
from jax import lax
from jax.experimental import pallas as pl
from jax.experimental.pallas import tpu as pltpu

F32 = jnp.float32
BF16 = jnp.bfloat16

D_MODEL = 4096
M_HEADS = 8
M_V_DIM = D_MODEL // (2 * M_HEADS)
M_QK_DIM = M_V_DIM // 2
M_WIDTH = M_HEADS * M_V_DIM
QK_WIDTH = M_HEADS * M_QK_DIM
CONV_WIDTH = 4
S5_WIDTH = D_MODEL // 4
S5_GROUP = 16
S5_GROUPS = S5_WIDTH // S5_GROUP
S5_STATE = 64
S5_STATES = S5_GROUPS * S5_STATE
D_FF = 4 * D_MODEL
EPS = 1e-6

LANES = 128
SUBLANES = 8
VMEM_LIMIT = 56 * 1024 * 1024

PROJ_MAIN = 2 * QK_WIDTH + 2 * M_WIDTH + S5_WIDTH + 2 * D_MODEL
COL_V = 2 * QK_WIDTH
COL_O = COL_V + M_WIDTH
COL_U = COL_O + M_WIDTH
COL_GA = COL_U + S5_WIDTH
COL_GB = COL_GA + D_MODEL

MLSTM_CHUNK = 256
S5_BLOCK = 512
S5_SUB = 32
S5_NSUB = S5_BLOCK // S5_SUB
S5_TILE_STATES = 512
S5_TILES = S5_WIDTH // LANES


def _params(*sem):
    return pltpu.CompilerParams(dimension_semantics=sem, vmem_limit_bytes=VMEM_LIMIT)


def _sigmoid(x):
    return 1.0 / (1.0 + jnp.exp(-x))


def _rmsnorm_kernel(x_ref, g_ref, o_ref):
    x = x_ref[...]
    ms = jnp.mean(x * x, axis=-1, keepdims=True)
    o_ref[...] = (x * lax.rsqrt(ms + EPS) * g_ref[...]).astype(o_ref.dtype)


def _rmsnorm(x, gain, tm=256):
    t, d = x.shape
    return pl.pallas_call(
        _rmsnorm_kernel,
        grid=(t // tm,),
        in_specs=[pl.BlockSpec((tm, d), lambda i: (i, 0)),
                  pl.BlockSpec((1, d), lambda i: (0, 0))],
        out_specs=pl.BlockSpec((tm, d), lambda i: (i, 0)),
        out_shape=jax.ShapeDtypeStruct((t, d), BF16),
        compiler_params=_params("parallel"),
        name="rmsnorm",
    )(x, gain.reshape(1, d))


def _norm_residual_kernel(y_ref, x_ref, gpost_ref, gnext_ref, xo_ref, hn_ref):
    y = y_ref[...].astype(F32)
    ms = jnp.mean(y * y, axis=-1, keepdims=True)
    xn = x_ref[...] + y * lax.rsqrt(ms + EPS) * gpost_ref[...]
    xo_ref[...] = xn
    if hn_ref is not None:
        ms2 = jnp.mean(xn * xn, axis=-1, keepdims=True)
        hn_ref[...] = (xn * lax.rsqrt(ms2 + EPS) * gnext_ref[...]).astype(hn_ref.dtype)


def _norm_residual_last_kernel(y_ref, x_ref, gpost_ref, xo_ref):
    _norm_residual_kernel(y_ref, x_ref, gpost_ref, None, xo_ref, None)


def _norm_residual(y, x, g_post, g_next, tm=256):
    t, d = x.shape
    row = pl.BlockSpec((tm, d), lambda i: (i, 0))
    vec = pl.BlockSpec((1, d), lambda i: (0, 0))
    if g_next is None:
        xo = pl.pallas_call(
            _norm_residual_last_kernel, grid=(t // tm,),
            in_specs=[row, row, vec], out_specs=row,
            out_shape=jax.ShapeDtypeStruct((t, d), F32),
            compiler_params=_params("parallel"),
            name="norm_residual_last",
        )(y, x, g_post.reshape(1, d))
        return xo, None
    return pl.pallas_call(
        _norm_residual_kernel, grid=(t // tm,),
        in_specs=[row, row, vec, vec], out_specs=[row, row],
        out_shape=[jax.ShapeDtypeStruct((t, d), F32), jax.ShapeDtypeStruct((t, d), BF16)],
        compiler_params=_params("parallel"),
        name="norm_residual",
    )(y, x, g_post.reshape(1, d), g_next.reshape(1, d))


def _log_sigmoid(z):
    return jnp.minimum(z, 0.0) - jnp.log1p(jnp.exp(-jnp.abs(z)))


W_IN_SKIP = 2 * M_HEADS


def _cast_w_in_kernel(wt_ref, o_ref):
    o_ref[...] = wt_ref[0].T.astype(o_ref.dtype)


def _cast_w_in(w_in_t, layer, tn=512, tk=2048):
    d = w_in_t.shape[2]

    def src_row(j):
        start = j * tn
        return pl.multiple_of(start + jnp.where(start >= COL_U, W_IN_SKIP, 0), W_IN_SKIP)

    return pl.pallas_call(
        _cast_w_in_kernel,
        grid=(PROJ_MAIN // tn, d // tk),
        in_specs=[pl.BlockSpec((pl.Element(1), pl.Element(tn), pl.Element(tk)),
                               lambda j, k: (layer, src_row(j), k * tk))],
        out_specs=pl.BlockSpec((tk, tn), lambda j, k: (k, j)),
        out_shape=jax.ShapeDtypeStruct((d, PROJ_MAIN), BF16),
        compiler_params=_params("parallel", "parallel"),
        name="cast_w_in",
    )(w_in_t)


FF2_TILE = 256


def _matmul_kernel(a_ref, w_ref, *rest, relu2, ride):
    if ride:
        w2_ref, o_ref, w2b_ref = rest
        for jt in range(w2b_ref.shape[0]):
            w2b_ref[jt] = w2_ref[:, jt * FF2_TILE:(jt + 1) * FF2_TILE].astype(BF16)
    else:
        (o_ref,) = rest
    acc = jnp.dot(a_ref[...], w_ref[...], preferred_element_type=F32)
    if relu2:
        acc = jnp.square(jnp.maximum(acc, 0.0))
    o_ref[...] = acc.astype(o_ref.dtype)


def _matmul(a, w, tm, tn, name, relu2=False, ride_stack=None, layer=0):
    m, k = a.shape
    if w.ndim == 3:
        n = w.shape[0] * tn
        w_spec = pl.BlockSpec((None, k, tn), lambda i, j: (j, 0, 0))
    else:
        n = w.shape[1]
        w_spec = pl.BlockSpec((k, tn), lambda i, j: (0, j))
    ni, nj = m // tm, n // tn
    in_specs = [pl.BlockSpec((tm, k), lambda i, j: (i, 0)), w_spec]
    out_specs = [pl.BlockSpec((tm, tn), lambda i, j: (i, j))]
    out_shape = [jax.ShapeDtypeStruct((m, n), BF16)]
    args = [a, w]
    if ride_stack is not None:
        r, c = ride_stack.shape[1:]
        rb = r // (ni * nj)
        assert rb * ni * nj == r and rb % (2 * SUBLANES) == 0
        nt = c // FF2_TILE
        in_specs.append(pl.BlockSpec((None, rb, c), lambda i, j: (layer, i * nj + j, 0)))
        out_specs.append(pl.BlockSpec((nt, rb, FF2_TILE), lambda i, j: (0, i * nj + j, 0)))
        out_shape.append(jax.ShapeDtypeStruct((nt, r, FF2_TILE), BF16))
        args.append(ride_stack)
    out = pl.pallas_call(
        functools.partial(_matmul_kernel, relu2=relu2, ride=ride_stack is not None),
        grid=(ni, nj),
        in_specs=in_specs, out_specs=out_specs, out_shape=out_shape,
        compiler_params=_params("parallel", "parallel"),
        name=name,
    )(*args)
    return out if ride_stack is not None else out[0]


def _matmul_wcast_kernel(a_ref, w_ref, ride_ref, o_ref, rideb_ref, wb_scr):
    rideb_ref[...] = ride_ref[...].astype(BF16)

    @pl.when(pl.program_id(1) == 0)
    def _():
        wb_scr[...] = w_ref[...].astype(BF16)

    o_ref[...] = jnp.dot(a_ref[...], wb_scr[...], preferred_element_type=F32).astype(o_ref.dtype)


def _matmul_wcast(a, w_stack, ride_stack, layer, name, tm=1024, tn=512):
    m, k = a.shape
    n = w_stack.shape[2]
    nj, ni = n // tn, m // tm
    r, c = ride_stack.shape[1:]
    rb = r // (nj * ni)
    assert rb * nj * ni == r and rb % (2 * SUBLANES) == 0
    return pl.pallas_call(
        _matmul_wcast_kernel,
        grid=(nj, ni),
        in_specs=[pl.BlockSpec((tm, k), lambda j, i: (i, 0)),
                  pl.BlockSpec((None, k, tn), lambda j, i: (layer, 0, j)),
                  pl.BlockSpec((None, rb, c), lambda j, i: (layer, j * ni + i, 0))],
        out_specs=[pl.BlockSpec((tm, tn), lambda j, i: (i, j)),
                   pl.BlockSpec((rb, c), lambda j, i: (j * ni + i, 0))],
        out_shape=[jax.ShapeDtypeStruct((m, n), BF16), jax.ShapeDtypeStruct((r, c), BF16)],
        scratch_shapes=[pltpu.VMEM((k, tn), BF16)],
        compiler_params=_params("parallel", "arbitrary"),
        name=name,
    )(a, w_stack, ride_stack)


def _cumsum_rows(x):
    n = x.shape[0]
    row = lax.broadcasted_iota(jnp.int32, x.shape, 0)
    k = 1
    while k < n:
        x = x + jnp.where(row >= k, pltpu.roll(x, k, 0), 0.0)
        k *= 2
    return x


def _cummax_rows(x):
    n = x.shape[0]
    row = lax.broadcasted_iota(jnp.int32, x.shape, 0)
    k = 1
    while k < n:
        x = jnp.maximum(x, jnp.where(row >= k, pltpu.roll(x, k, 0), -jnp.inf))
        k *= 2
    return x


def _mlstm_kernel(xn_ref, wg_ref, bias_ref, qk_ref, v_ref, o_ref, cw_ref, gain_ref, out_ref,
                  wgb_scr, c_scr, n_scr, m_scr, tail_scr):
    L = qk_ref.shape[0]

    @pl.when(pl.program_id(1) == 0)
    def _():
        wgb_scr[...] = wg_ref[...].astype(BF16)
        c_scr[...] = jnp.zeros_like(c_scr)
        n_scr[...] = jnp.zeros_like(n_scr)
        m_scr[...] = jnp.zeros_like(m_scr)
        tail_scr[...] = jnp.zeros_like(tail_scr)

    x = qk_ref[...].astype(F32)
    tail = tail_scr[...]
    row8 = lax.broadcasted_iota(jnp.int32, tail.shape, 0)
    acc = x * cw_ref[CONV_WIDTH - 1:CONV_WIDTH, :]
    for sh in range(1, CONV_WIDTH):
        r = pltpu.roll(x, sh, 0)
        head = jnp.where(row8 < sh, pltpu.roll(tail, sh, 0), r[0:SUBLANES])
        r = jnp.concatenate([head, r[SUBLANES:]], axis=0)
        acc = acc + r * cw_ref[CONV_WIDTH - 1 - sh:CONV_WIDTH - sh, :]
    tail_scr[...] = x[L - SUBLANES:L]
    qk = acc * _sigmoid(acc)
    q_all = (qk[:, :QK_WIDTH] * (M_QK_DIM ** -0.5)).astype(BF16)
    k_all = qk[:, QK_WIDTH:]

    z = lax.dot_general(xn_ref[...], wgb_scr[...], (((1,), (1,)), ((), ())),
                        preferred_element_type=F32) + bias_ref[...]
    lane = lax.broadcasted_iota(jnp.int32, z.shape, 1)
    used = lane < M_HEADS
    log_i = jnp.where(used, z, 0.0)
    log_f = jnp.where(used, pltpu.roll(_log_sigmoid(z), LANES - M_HEADS, 1), 0.0)
    b = _cumsum_rows(log_f)
    r = log_i - b
    m_prev = m_scr[0:1, :]
    m_t = b + jnp.maximum(m_prev, _cummax_rows(r))
    col_all = b - m_t
    sc_all = jnp.exp(b + m_prev - m_t)
    eneg_all = jnp.exp(-m_t)
    g_tot = b[L - 1:L, :]
    a_all = g_tot + r
    m_new = jnp.maximum(g_tot + m_prev, jnp.max(a_all, axis=0, keepdims=True))
    wk_all = jnp.exp(a_all - m_new)
    decay_all = jnp.exp(g_tot + m_prev - m_new)
    m_scr[0:1, :] = m_new
    r_rows = r.T

    ti = lax.broadcasted_iota(jnp.int32, (L, L), 0)
    si = lax.broadcasted_iota(jnp.int32, (L, L), 1)
    causal = ti >= si

    for h in range(M_HEADS):
        qh = q_all[:, h * M_QK_DIM:(h + 1) * M_QK_DIM]
        kh = k_all[:, h * M_QK_DIM:(h + 1) * M_QK_DIM]
        vh = v_ref[:, h * M_V_DIM:(h + 1) * M_V_DIM]
        c_prev = c_scr[h]
        n_prev = n_scr[h:h + 1, :]
        sc = sc_all[:, h:h + 1]

        p = jnp.exp(jnp.where(causal, col_all[:, h:h + 1] + r_rows[h:h + 1, :], -jnp.inf))
        s = lax.dot_general(qh, kh.astype(BF16), (((1,), (1,)), ((), ())),
                            preferred_element_type=F32) * p
        num = (jnp.dot(s.astype(BF16), vh, preferred_element_type=F32)
               + sc * jnp.dot(qh, c_prev.astype(BF16), preferred_element_type=F32))
        den = (jnp.sum(s, axis=1, keepdims=True)
               + sc * jnp.sum(qh.astype(F32) * n_prev, axis=1, keepdims=True))
        inv = 1.0 / jnp.maximum(jnp.abs(den), eneg_all[:, h:h + 1])

        kw = kh * wk_all[:, h:h + 1]
        decay = decay_all[:, h:h + 1]
        c_scr[h] = decay * c_prev + lax.dot_general(
            kw.astype(BF16), vh, (((0,), (0,)), ((), ())), preferred_element_type=F32)
        n_scr[h:h + 1, :] = decay * n_prev + jnp.sum(kw, axis=0, keepdims=True)

        mu = jnp.mean(num, axis=1, keepdims=True)
        hc = num - mu
        var = jnp.mean(hc * hc, axis=1, keepdims=True)
        scale = inv * lax.rsqrt(inv * inv * var + EPS)
        cols = slice(h * M_V_DIM, (h + 1) * M_V_DIM)
        hn = hc * scale * (gain_ref[:, cols] * _sigmoid(o_ref[:, cols].astype(F32)))
        out_ref[:, cols] = hn.astype(out_ref.dtype)


def _mlstm(xn, w_in_t, layer, bias, proj, conv_w, gain, batch, seq):
    L = MLSTM_CHUNK
    nc = seq // L
    d = xn.shape[1]
    row = lambda b, c: b * nc + c
    return pl.pallas_call(
        _mlstm_kernel,
        grid=(batch, nc),
        in_specs=[
            pl.BlockSpec((L, d), lambda b, c: (row(b, c), 0)),
            pl.BlockSpec((None, LANES, d), lambda b, c: (layer, COL_U // LANES, 0)),
            pl.BlockSpec((1, LANES), lambda b, c: (0, 0)),
            pl.BlockSpec((L, 2 * QK_WIDTH), lambda b, c: (row(b, c), 0)),
            pl.BlockSpec((L, M_WIDTH), lambda b, c: (row(b, c), COL_V // M_WIDTH)),
            pl.BlockSpec((L, M_WIDTH), lambda b, c: (row(b, c), COL_O // M_WIDTH)),
            pl.BlockSpec((CONV_WIDTH, 2 * QK_WIDTH), lambda b, c: (0, 0)),
            pl.BlockSpec((1, M_WIDTH), lambda b, c: (0, 0)),
        ],
        out_specs=pl.BlockSpec((L, M_WIDTH), lambda b, c: (row(b, c), 0)),
        out_shape=jax.ShapeDtypeStruct((batch * seq, M_WIDTH), BF16),
        scratch_shapes=[
            pltpu.VMEM((LANES, d), BF16),
            pltpu.VMEM((M_HEADS, M_QK_DIM, M_V_DIM), F32),
            pltpu.VMEM((M_HEADS, M_QK_DIM), F32),
            pltpu.VMEM((SUBLANES, LANES), F32),
            pltpu.VMEM((SUBLANES, 2 * QK_WIDTH), F32),
        ],
        compiler_params=_params("parallel", "arbitrary"),
        name="mlstm",
    )(xn, w_in_t, bias, proj, proj, proj, conv_w, gain.reshape(1, M_WIDTH))


def _s5_kernel(u_ref, perm_ref, permt_ref, bbar_ref, cmat_ref, lam_ref, lamsub_ref, pow_ref,
               d_ref, wglu_ref, bglu_ref, out_ref, st_scr, x0_scr, bu_scr, xs_scr, y_scr):
    ns, nsub, w = S5_SUB, S5_NSUB, S5_TILE_STATES

    @pl.when(pl.program_id(1) == 0)
    def _():
        st_scr[...] = jnp.zeros_like(st_scr)

    up = jnp.dot(perm_ref[...], u_ref[...], preferred_element_type=F32).astype(BF16)

    for j in range(S5_TILES):
        ut = up[:, j * LANES:(j + 1) * LANES]
        bu_scr[...] = jnp.dot(ut, bbar_ref[j], preferred_element_type=F32)
        lr = lam_ref[0:1, j * w:(j + 1) * w]
        li = lam_ref[1:2, j * w:(j + 1) * w]

        def local_step(s, carry):
            xr, xi = carry
            rows = pl.ds(pl.multiple_of(s * nsub, nsub), nsub)
            nr = lr * xr - li * xi + bu_scr[rows, 0:w]
            ni = lr * xi + li * xr + bu_scr[rows, w:2 * w]
            bu_scr[rows, 0:w] = nr
            bu_scr[rows, w:2 * w] = ni
            return nr, ni

        zero = jnp.zeros((nsub, w), F32)
        er, ei = lax.fori_loop(0, ns, local_step, (zero, zero), unroll=True)

        x0_scr[0:nsub, :] = er
        x0_scr[nsub:2 * nsub, :] = ei
        lsr = lamsub_ref[0:1, j * w:(j + 1) * w]
        lsi = lamsub_ref[1:2, j * w:(j + 1) * w]

        def chain_step(c, carry):
            xr, xi = carry
            e_r = x0_scr[pl.ds(c, 1), :]
            e_i = x0_scr[pl.ds(nsub + c, 1), :]
            x0_scr[pl.ds(c, 1), :] = xr
            x0_scr[pl.ds(nsub + c, 1), :] = xi
            return lsr * xr - lsi * xi + e_r, lsr * xi + lsi * xr + e_i

        sr, si = lax.fori_loop(0, nsub, chain_step,
                               (st_scr[0:1, j * w:(j + 1) * w], st_scr[1:2, j * w:(j + 1) * w]),
                               unroll=True)
        st_scr[0:1, j * w:(j + 1) * w] = sr
        st_scr[1:2, j * w:(j + 1) * w] = si
        x0r = x0_scr[0:nsub, :]
        x0i = x0_scr[nsub:2 * nsub, :]

        def fix_step(s, carry):
            rows = pl.ds(pl.multiple_of(s * nsub, nsub), nsub)
            pr = pow_ref[pl.ds(s, 1), j * w:(j + 1) * w]
            pi = pow_ref[pl.ds(ns + s, 1), j * w:(j + 1) * w]
            xs_scr[rows, 0:w] = (bu_scr[rows, 0:w] + pr * x0r - pi * x0i).astype(BF16)
            xs_scr[rows, w:2 * w] = (bu_scr[rows, w:2 * w] + pr * x0i + pi * x0r).astype(BF16)
            return carry

        lax.fori_loop(0, ns, fix_step, 0, unroll=True)
        y_scr[:, j * LANES:(j + 1) * LANES] = (
            jnp.dot(xs_scr[...], cmat_ref[j], preferred_element_type=F32)
            + d_ref[:, j * LANES:(j + 1) * LANES] * ut.astype(F32))

    y = y_scr[...]
    z = 0.5 * y * (1.0 + jnp.tanh(math.sqrt(2.0 / math.pi) * (y + 0.044715 * (y * y * y))))
    gate = jnp.dot(z.astype(BF16), wglu_ref[...], preferred_element_type=F32) + bglu_ref[...]
    zg = (z * _sigmoid(gate)).astype(BF16)
    out_ref[...] = jnp.dot(permt_ref[...], zg, preferred_element_type=F32).astype(out_ref.dtype)


def _s5(proj, perm, permt, bbar, cmat, lam, lamsub, pows, d_skip, w_glu, b_glu, batch, seq):
    tb = S5_BLOCK
    nb = seq // tb
    full = lambda a: pl.BlockSpec(a.shape, lambda b, c: (0,) * a.ndim)
    return pl.pallas_call(
        _s5_kernel,
        grid=(batch, nb),
        in_specs=[pl.BlockSpec((tb, S5_WIDTH), lambda b, c: (b * nb + c, COL_U // S5_WIDTH)),
                  full(perm), full(permt), full(bbar), full(cmat), full(lam), full(lamsub),
                  full(pows), full(d_skip), full(w_glu), full(b_glu)],
        out_specs=pl.BlockSpec((tb, S5_WIDTH), lambda b, c: (b * nb + c, 0)),
        out_shape=jax.ShapeDtypeStruct((batch * seq, S5_WIDTH), BF16),
        scratch_shapes=[
            pltpu.VMEM((2, S5_STATES), F32),
            pltpu.VMEM((2 * S5_NSUB, S5_TILE_STATES), F32),
            pltpu.VMEM((tb, 2 * S5_TILE_STATES), F32),
            pltpu.VMEM((tb, 2 * S5_TILE_STATES), BF16),
            pltpu.VMEM((tb, S5_WIDTH), F32),
        ],
        compiler_params=_params("parallel", "arbitrary"),
        name="s5",
    )(proj, perm, permt, bbar, cmat, lam, lamsub, pows, d_skip, w_glu, b_glu)


def _s5_tables(a_re, a_im, log_step, b_re, b_im, c_re, c_im):
    lam_re = jnp.minimum(a_re, -1e-4)
    lam_im = a_im
    dt = jnp.exp(log_step)[:, None]
    mag = jnp.exp(lam_re * dt)
    bar_re = mag * jnp.cos(lam_im * dt)
    bar_im = mag * jnp.sin(lam_im * dt)
    den = lam_re * lam_re + lam_im * lam_im
    nr = bar_re - 1.0
    r_re = (nr * lam_re + bar_im * lam_im) / den
    r_im = (bar_im * lam_re - nr * lam_im) / den
    bb_re = r_re[..., None] * b_re - r_im[..., None] * b_im
    bb_im = r_re[..., None] * b_im + r_im[..., None] * b_re

    gpt = S5_GROUPS // S5_TILES
    eye = jnp.eye(gpt, dtype=F32)

    def in_tile(bb):
        t = bb.reshape(S5_TILES, gpt, S5_STATE, S5_GROUP)
        t = jnp.einsum('jgpn,gh->jgnhp', t, eye)
        return t.reshape(S5_TILES, LANES, S5_TILE_STATES)

    def out_tile(c):
        t = c.reshape(S5_TILES, gpt, S5_GROUP, S5_STATE)
        t = jnp.einsum('jgnp,gh->jgphn', t, eye)
        return t.reshape(S5_TILES, S5_TILE_STATES, LANES)

    bbar = jnp.concatenate([in_tile(bb_re), in_tile(bb_im)], axis=2).astype(BF16)
    cmat = jnp.concatenate([out_tile(c_re), -out_tile(c_im)], axis=1).astype(BF16)
    lam = jnp.stack([bar_re.reshape(-1), bar_im.reshape(-1)])

    def power(k):
        mk = jnp.exp(k * lam_re * dt)
        return (mk * jnp.cos(k * lam_im * dt)).reshape(-1), (mk * jnp.sin(k * lam_im * dt)).reshape(-1)

    lamsub = jnp.stack(power(float(S5_SUB)))
    steps = [power(float(s + 1)) for s in range(S5_SUB)]
    pows = jnp.stack([p[0] for p in steps] + [p[1] for p in steps])
    return bbar, cmat, lam, lamsub, pows


def _s5_perm():
    r = jnp.arange(S5_BLOCK)
    src = (r % S5_NSUB) * S5_SUB + r // S5_NSUB
    perm = (src[:, None] == jnp.arange(S5_BLOCK)[None, :]).astype(BF16)
    return perm, perm.T


def _merge_kernel(ha_ref, zb_ref, pa_ref, pb_ref, ga_ref, gb_ref, o_ref, pab_scr, pbb_scr):
    @pl.when(pl.program_id(1) == 0)
    def _():
        pab_scr[...] = pa_ref[...].astype(BF16)
        pbb_scr[...] = pb_ref[...].astype(BF16)

    ya = jnp.dot(ha_ref[...], pab_scr[...], preferred_element_type=F32)
    yb = jnp.dot(zb_ref[...], pbb_scr[...], preferred_element_type=F32)
    sa = _sigmoid(ga_ref[...].astype(F32))
    sb = _sigmoid(gb_ref[...].astype(F32))
    o_ref[...] = (sa * ya + sb * yb).astype(o_ref.dtype)


def _merge(ha, zb, pa_stack, pb_stack, layer, proj, tm=512, tn=1024):
    t = ha.shape[0]
    return pl.pallas_call(
        _merge_kernel,
        grid=(D_MODEL // tn, t // tm),
        in_specs=[pl.BlockSpec((tm, M_WIDTH), lambda j, i: (i, 0)),
                  pl.BlockSpec((tm, S5_WIDTH), lambda j, i: (i, 0)),
                  pl.BlockSpec((None, M_WIDTH, tn), lambda j, i: (layer, 0, j)),
                  pl.BlockSpec((None, S5_WIDTH, tn), lambda j, i: (layer, 0, j)),
                  pl.BlockSpec((tm, tn), lambda j, i: (i, COL_GA // tn + j)),
                  pl.BlockSpec((tm, tn), lambda j, i: (i, COL_GB // tn + j))],
        out_specs=pl.BlockSpec((tm, tn), lambda j, i: (i, j)),
        out_shape=jax.ShapeDtypeStruct((t, D_MODEL), BF16),
        scratch_shapes=[pltpu.VMEM((M_WIDTH, tn), BF16), pltpu.VMEM((S5_WIDTH, tn), BF16)],
        compiler_params=_params("parallel", "arbitrary"),
        name="merge",
    )(ha, zb, pa_stack, pb_stack, proj, proj)


def kernel(x, pre_mix_gain, post_mix_gain, pre_mlp_gain, post_mlp_gain, w_in, conv_qk, b_igate, b_fgate, mh_norm_gain, p_a, s5_a_re, s5_a_im, s5_log_step, s5_b_re, s5_b_im, s5_c_re, s5_c_im, s5_d, w_glu, b_glu, p_b, w_out, w_ff1, w_ff2):
    batch, seq, d = x.shape
    depth = w_in.shape[0]
    t = batch * seq
    xf = x.reshape(t, d)
    perm, permt = _s5_perm()
    w_in_t = jnp.swapaxes(w_in, 1, 2)
    xn = _rmsnorm(xf, pre_mix_gain[0])
    for l in range(depth):
        bias = jnp.concatenate([b_igate[l], b_fgate[l]])
        bias = jnp.pad(bias, (0, LANES - 2 * M_HEADS)).reshape(1, LANES)

        proj = _matmul(xn, _cast_w_in(w_in_t, l), 1024, 1024, "proj")
        ha = _mlstm(xn, w_in_t, l, bias, proj, conv_qk[l], mh_norm_gain[l], batch, seq)
        tables = _s5_tables(s5_a_re[l], s5_a_im[l], s5_log_step[l], s5_b_re[l], s5_b_im[l],
                            s5_c_re[l], s5_c_im[l])
        zb = _s5(proj, perm, permt, *tables, s5_d[l].reshape(1, S5_WIDTH), w_glu[l].astype(BF16),
                 b_glu[l].reshape(1, S5_WIDTH), batch, seq)
        merged = _merge(ha, zb, p_a, p_b, l, proj)
        mix, w1b = _matmul_wcast(merged, w_out, w_ff1, l, "w_out")
        xf, hn = _norm_residual(mix, xf, post_mix_gain[l], pre_mlp_gain[l])
        hid, w2b = _matmul(hn, w1b, 1024, 1024, "ff1", relu2=True, ride_stack=w_ff2, layer=l)
        ff = _matmul(hid, w2b, 512, FF2_TILE, "ff2")
        g_next = pre_mix_gain[l + 1] if l + 1 < depth else None
        xf, xn = _norm_residual(ff, xf, post_mlp_gain[l], g_next)
    return xf.reshape(batch, seq, d)
```
